```python
import jax, jax.numpy as jnp
from jax import lax
import numpy as np

D_MODEL = 2048
BATCH = 1
SEQ = 16384
DEPTH = 1

CHUNK = 64
SGU_BLOCK = 128
SGU_HEADS = 8
SGU_WIDTH = D_MODEL
SGU_HEAD_DIM = SGU_WIDTH // SGU_HEADS
POOL_WINDOWS = (2, 4, 8, 16)
POOL_GROUPS = len(POOL_WINDOWS)
POOL_WIDTH = D_MODEL
POOL_GROUP_DIM = POOL_WIDTH // POOL_GROUPS
N_BRANCHES = 2
D_FF = 5632
IN_COLS = 2 * SGU_WIDTH + POOL_WIDTH + N_BRANCHES * D_MODEL
RMS_EPS = 1e-6
LN_EPS = 1e-5

kernel_name = "hybrid_sgu_pool_gated_macaron"


def _rmsnorm(x, g):
    xf = x.astype(jnp.float32)
    y = xf * lax.rsqrt(jnp.mean(xf * xf, axis=-1, keepdims=True) + RMS_EPS)
    return (y * g.astype(jnp.float32)).astype(x.dtype)


def _layernorm(x, g, b):
    xf = x.astype(jnp.float32)
    mu = jnp.mean(xf, axis=-1, keepdims=True)
    var = jnp.mean(jnp.square(xf - mu), axis=-1, keepdims=True)
    y = (xf - mu) * lax.rsqrt(var + LN_EPS)
    return (y * g.astype(jnp.float32) + b.astype(jnp.float32)).astype(x.dtype)


def _swiglu(x, w_in, w_out):
    gate, up = jnp.split(x @ w_in, 2, axis=-1)
    return (jax.nn.silu(gate) * up) @ w_out


def _spatial_gating(z, ln_g, ln_b, w_s, b_s):
    bsz, s_len, _ = z.shape
    u, v = jnp.split(jax.nn.gelu(z, approximate=False), 2, axis=-1)
    v = _layernorm(v, ln_g, ln_b)
    v = v.reshape(bsz, s_len // SGU_BLOCK, SGU_BLOCK, SGU_HEADS, SGU_HEAD_DIM)
    pos = jnp.arange(SGU_BLOCK)
    mask = (pos[None, :] // CHUNK) <= (pos[:, None] // CHUNK)
    w = w_s * mask.astype(w_s.dtype)[None]
    s = jnp.einsum('hij,bnjhd->bnihd', w, v) + b_s.T[:, :, None]
    return u * s.reshape(bsz, s_len, SGU_WIDTH)


def _multiscale_pool(p, pool_w, pool_scale):
    bsz, s_len, _ = p.shape
    pf = p.astype(jnp.float32)
    csum = jnp.concatenate([jnp.zeros((bsz, 1, POOL_WIDTH), jnp.float32),
                            jnp.cumsum(pf, axis=1)], axis=1)
    upper = csum[:, 1:]
    t = jnp.arange(s_len)
    outs = []
    for k, w in enumerate(POOL_WINDOWS):
        sl = slice(k * POOL_GROUP_DIM, (k + 1) * POOL_GROUP_DIM)
        lower = jnp.pad(csum[:, :s_len + 1 - w, sl], ((0, 0), (w - 1, 0), (0, 0)))
        cnt = jnp.minimum(t + 1, w).astype(jnp.float32)[None, :, None]
        outs.append((upper[..., sl] - lower) / cnt - pf[..., sl])
    pooled = jnp.stack(outs, axis=2).astype(p.dtype)
    mixed = jnp.einsum('bsgc,gcd->bsgd', pooled, pool_w)
    return mixed.reshape(bsz, s_len, POOL_WIDTH) * pool_scale


def setup_inputs(seed: int = 0) -> dict:
    key = jax.random.key(seed)
    ks = jax.random.split(key, 24)
    f32 = jnp.float32

    def nrm(k, shape, fan_in):
        return jax.random.normal(k, shape, f32) * (fan_in ** -0.5)

    def gain(k, shape):
        return 1.0 + 0.02 * jax.random.normal(k, shape, f32)

    return {
        "x": jax.random.normal(ks[0], (BATCH, SEQ, D_MODEL), f32),
        "ffn1_norm": gain(ks[1], (D_MODEL,)),
        "ffn1_w_in": nrm(ks[2], (D_MODEL, 2 * D_FF), D_MODEL),
        "ffn1_w_out": nrm(ks[3], (D_FF, D_MODEL), D_FF),
        "mix_norm": gain(ks[4], (D_MODEL,)),
        "w_in": nrm(ks[5], (D_MODEL, IN_COLS), D_MODEL),
        "b_in": 0.02 * jax.random.normal(ks[6], (IN_COLS,), f32),
        "sgu_ln_g": gain(ks[7], (SGU_WIDTH,)),
        "sgu_ln_b": 0.02 * jax.random.normal(ks[8], (SGU_WIDTH,), f32),
        "sgu_w_s": nrm(ks[9], (SGU_HEADS, SGU_BLOCK, SGU_BLOCK), SGU_BLOCK),
        "sgu_b_s": gain(ks[10], (SGU_HEADS, SGU_BLOCK)),
        "pool_w": nrm(ks[11], (POOL_GROUPS, POOL_GROUP_DIM, POOL_GROUP_DIM), POOL_GROUP_DIM),
        "pool_scale": gain(ks[12], (POOL_WIDTH,)),
        "w_branch_a": nrm(ks[13], (SGU_WIDTH, D_MODEL), SGU_WIDTH),
        "w_branch_b": nrm(ks[14], (POOL_WIDTH, D_MODEL), POOL_WIDTH),
        "w_out": nrm(ks[15], (D_MODEL, D_MODEL), D_MODEL),
        "ffn2_norm": gain(ks[16], (D_MODEL,)),
        "ffn2_w_in": nrm(ks[17], (D_MODEL, 2 * D_FF), D_MODEL),
        "ffn2_w_out": nrm(ks[18], (D_FF, D_MODEL), D_FF),
        "final_norm": gain(ks[19], (D_MODEL,)),
    }


def reference(x, ffn1_norm, ffn1_w_in, ffn1_w_out, mix_norm, w_in, b_in,
              sgu_ln_g, sgu_ln_b, sgu_w_s, sgu_b_s, pool_w, pool_scale,
              w_branch_a, w_branch_b, w_out, ffn2_norm, ffn2_w_in, ffn2_w_out,
              final_norm):
    h = x
    for _ in range(DEPTH):
        h = h + 0.5 * _swiglu(_rmsnorm(h, ffn1_norm), ffn1_w_in, ffn1_w_out)

        n = _rmsnorm(h, mix_norm)
        proj = n @ w_in + b_in
        o_a = 2 * SGU_WIDTH
        o_b = o_a + POOL_WIDTH
        z_a = proj[..., :o_a]
        z_b = proj[..., o_a:o_b]
        gate_a = jax.nn.sigmoid(proj[..., o_b:o_b + D_MODEL])
        gate_b = jax.nn.sigmoid(proj[..., o_b + D_MODEL:])

        y_a = _spatial_gating(z_a, sgu_ln_g, sgu_ln_b, sgu_w_s, sgu_b_s) @ w_branch_a
        y_b = _multiscale_pool(z_b, pool_w, pool_scale) @ w_branch_b
        merged = gate_a * y_a + gate_b * y_b
        h = h + merged @ w_out

        h = h + 0.5 * _swiglu(_rmsnorm(h, ffn2_norm), ffn2_w_in, ffn2_w_out)
    return _rmsnorm(h, final_norm).astype(x.dtype)
```

```python
import functools

import jax
import jax.numpy as jnp
from jax import lax
from jax.experimental import pallas as pl
from jax.experimental.pallas import tpu as pltpu

D_MODEL = 2048
D_FF = 5632
CHUNK = 64
SGU_BLOCK = 128
SGU_HEADS = 8
SGU_HEAD_DIM = D_MODEL // SGU_HEADS
POOL_WINDOWS = (2, 4, 8, 16)
POOL_GROUP_DIM = D_MODEL // len(POOL_WINDOWS)
POOL_HALO = 16
RMS_EPS = 1e-6
LN_EPS = 1e-5

ROW_TILE = 512
COL_TILE = 512
V7X_VMEM_LIMIT_BYTES = 60 * 1024 * 1024

F32 = jnp.float32
BF16 = jnp.bfloat16


def _rms_bf16(x, g):
    y = x * lax.rsqrt(jnp.mean(x * x, axis=-1, keepdims=True) + RMS_EPS)
    return (y * g).astype(BF16)


def _gelu(x):
    return 0.5 * x * (1.0 + lax.erf(x * (2.0 ** -0.5)))


def _dot(a, b):
    return jnp.dot(a, b, preferred_element_type=F32)


def _params(semantics):
    return pltpu.CompilerParams(dimension_semantics=semantics,
                                vmem_limit_bytes=V7X_VMEM_LIMIT_BYTES)


def _ffn_kernel(x_ref, g_ref, wg_ref, wu_ref, wo_ref, fin_ref, o_ref, xn_ref, *, final_norm):
    j = pl.program_id(1)

    @pl.when(j == 0)
    def _():
        xn_ref[...] = _rms_bf16(x_ref[...], g_ref[...])

    xn = xn_ref[...]
    gate = _dot(xn, wg_ref[...])
    up = _dot(xn, wu_ref[...])
    act = (gate * jax.nn.sigmoid(gate) * up).astype(BF16)
    part = _dot(act, wo_ref[...])

    @pl.when(j == 0)
    def _():
        o_ref[...] = part

    @pl.when(j > 0)
    def _():
        o_ref[...] += part

    @pl.when(j == pl.num_programs(1) - 1)
    def _():
        h = x_ref[...] + 0.5 * o_ref[...]
        if final_norm:
            h = h * lax.rsqrt(jnp.mean(h * h, axis=-1, keepdims=True) + RMS_EPS) * fin_ref[...]
        o_ref[...] = h


def _ffn(x, norm_g, w_in, w_out, fin_g, *, final_norm):
    s = x.shape[0]
    n_ff = D_FF // COL_TILE
    row = lambda i, j: (i, 0)
    const = lambda i, j: (0, 0)
    return pl.pallas_call(
        functools.partial(_ffn_kernel, final_norm=final_norm),
        grid=(s // ROW_TILE, n_ff),
        in_specs=[
            pl.BlockSpec((ROW_TILE, D_MODEL), row),
            pl.BlockSpec((1, D_MODEL), const),
            pl.BlockSpec((D_MODEL, COL_TILE), lambda i, j: (0, j)),
            pl.BlockSpec((D_MODEL, COL_TILE), lambda i, j: (0, j + n_ff)),
            pl.BlockSpec((COL_TILE, D_MODEL), lambda i, j: (j, 0)),
            pl.BlockSpec((1, D_MODEL), const),
        ],
        out_specs=pl.BlockSpec((ROW_TILE, D_MODEL), row),
        out_shape=jax.ShapeDtypeStruct((s, D_MODEL), F32),
        scratch_shapes=[pltpu.VMEM((ROW_TILE, D_MODEL), BF16)],
        compiler_params=_params(("arbitrary", "arbitrary")),
        name="ffn_final" if final_norm else "ffn",
    )(x, norm_g, w_in, w_in, w_out, fin_g)


def _sgu_kernel(h_ref, g_ref, wu_ref, wv_ref, bu_ref, bv_ref, lng_ref, lnb_ref, ws_ref, bst_ref,
                a_ref, n_ref, u_ref, v_ref):
    j = pl.program_id(1)

    @pl.when(j == 0)
    def _():
        n_ref[...] = _rms_bf16(h_ref[...], g_ref[...])

    n = n_ref[...]
    col = pl.multiple_of(j * COL_TILE, COL_TILE)
    zu = _dot(n, wu_ref[...]) + bu_ref[...]
    zv = _dot(n, wv_ref[...]) + bv_ref[...]
    u_ref[:, pl.ds(col, COL_TILE)] = _gelu(zu)
    v_ref[:, pl.ds(col, COL_TILE)] = _gelu(zv)

    @pl.when(j == pl.num_programs(1) - 1)
    def _():
        v = v_ref[...]
        mu = jnp.mean(v, axis=-1, keepdims=True)
        var = jnp.mean(jnp.square(v - mu), axis=-1, keepdims=True)
        vln = ((v - mu) * lax.rsqrt(var + LN_EPS) * lng_ref[...] + lnb_ref[...]).astype(BF16)
        pi = lax.broadcasted_iota(jnp.int32, (SGU_BLOCK, SGU_BLOCK), 0) // CHUNK
        pj = lax.broadcasted_iota(jnp.int32, (SGU_BLOCK, SGU_BLOCK), 1) // CHUNK
        mask = (pj <= pi).astype(F32)
        bst = bst_ref[...]
        for h in range(SGU_HEADS):
            w = (ws_ref[h] * mask).astype(BF16)
            cs = slice(h * SGU_HEAD_DIM, (h + 1) * SGU_HEAD_DIM)
            bias = bst[:, h:h + 1]
            for blk in range(ROW_TILE // SGU_BLOCK):
                rs = slice(blk * SGU_BLOCK, (blk + 1) * SGU_BLOCK)
                sp = _dot(w, vln[rs, cs]) + bias
                a_ref[rs, cs] = (u_ref[rs, cs] * sp).astype(BF16)


def _sgu(h, norm_g, w_in, b_in, ln_g, ln_b, w_s, b_s_t):
    s = h.shape[0]
    n_c = D_MODEL // COL_TILE
    row = lambda i, j: (i, 0)
    const = lambda i, j: (0, 0)
    return pl.pallas_call(
        _sgu_kernel,
        grid=(s // ROW_TILE, n_c),
        in_specs=[
            pl.BlockSpec((ROW_TILE, D_MODEL), row),
            pl.BlockSpec((1, D_MODEL), const),
            pl.BlockSpec((D_MODEL, COL_TILE), lambda i, j: (0, j)),
            pl.BlockSpec((D_MODEL, COL_TILE), lambda i, j: (0, j + n_c)),
            pl.BlockSpec((1, COL_TILE), lambda i, j: (0, j)),
            pl.BlockSpec((1, COL_TILE), lambda i, j: (0, j + n_c)),
            pl.BlockSpec((1, D_MODEL), const),
            pl.BlockSpec((1, D_MODEL), const),
            pl.BlockSpec((SGU_HEADS, SGU_BLOCK, SGU_BLOCK), lambda i, j: (0, 0, 0)),
            pl.BlockSpec((SGU_BLOCK, SGU_HEADS), const),
        ],
        out_specs=pl.BlockSpec((ROW_TILE, D_MODEL), row),
        out_shape=jax.ShapeDtypeStruct((s, D_MODEL), BF16),
        scratch_shapes=[pltpu.VMEM((ROW_TILE, D_MODEL), BF16),
                        pltpu.VMEM((ROW_TILE, D_MODEL), F32),
                        pltpu.VMEM((ROW_TILE, D_MODEL), F32)],
        compiler_params=_params(("arbitrary", "arbitrary")),
        name="sgu",
    )(h, norm_g, w_in, w_in, b_in, b_in, ln_g, ln_b, w_s, b_s_t)


def _pool_kernel(h_ref, g_ref, w_ref, b_ref, pw_ref, ps_ref, o_ref, z_ref, *, seq):
    t0 = (pl.program_id(0) * ROW_TILE) % seq

    @pl.when(t0 == 0)
    def _():
        z_ref[0:POOL_HALO, :] = jnp.zeros((POOL_HALO, D_MODEL), F32)

    n = _rms_bf16(h_ref[...], g_ref[...])
    t = t0 + lax.broadcasted_iota(jnp.int32, (ROW_TILE, 1), 0)
    for k, win in enumerate(POOL_WINDOWS):
        cs = slice(k * POOL_GROUP_DIM, (k + 1) * POOL_GROUP_DIM)
        z = _dot(n, w_ref[:, cs]) + b_ref[:, cs]
        z_ref[POOL_HALO:, cs] = z
        acc = z
        for d in range(1, win):
            acc = acc + z_ref[POOL_HALO - d:POOL_HALO - d + ROW_TILE, cs]
        inv_cnt = 1.0 / jnp.minimum(t + 1, win).astype(F32)
        pooled = (acc * inv_cnt - z).astype(BF16)
        mixed = _dot(pooled, pw_ref[k])
        o_ref[:, cs] = (mixed * ps_ref[:, cs]).astype(BF16)
    z_ref[0:POOL_HALO, :] = z_ref[ROW_TILE:ROW_TILE + POOL_HALO, :]


def _pool(h, norm_g, w_in, b_in, pool_w, pool_scale, *, seq):
    s = h.shape[0]
    zb_block = 2 * D_MODEL // D_MODEL
    return pl.pallas_call(
        functools.partial(_pool_kernel, seq=seq),
        grid=(s // ROW_TILE,),
        in_specs=[
            pl.BlockSpec((ROW_TILE, D_MODEL), lambda i: (i, 0)),
            pl.BlockSpec((1, D_MODEL), lambda i: (0, 0)),
            pl.BlockSpec((D_MODEL, D_MODEL), lambda i: (0, zb_block)),
            pl.BlockSpec((1, D_MODEL), lambda i: (0, zb_block)),
            pl.BlockSpec((len(POOL_WINDOWS), POOL_GROUP_DIM, POOL_GROUP_DIM), lambda i: (0, 0, 0)),
            pl.BlockSpec((1, D_MODEL), lambda i: (0, 0)),
        ],
        out_specs=pl.BlockSpec((ROW_TILE, D_MODEL), lambda i: (i, 0)),
        out_shape=jax.ShapeDtypeStruct((s, D_MODEL), BF16),
        scratch_shapes=[pltpu.VMEM((ROW_TILE + POOL_HALO, D_MODEL), F32)],
        compiler_params=_params(("arbitrary",)),
        name="pool",
    )(h, norm_g, w_in, b_in, pool_w, pool_scale)


def _merge_kernel(h_ref, g_ref, a_ref, b_ref, wga_ref, wgb_ref, bga_ref, bgb_ref,
                  wa_ref, wb_ref, wo_ref, o_ref, n_ref):
    j = pl.program_id(1)

    @pl.when(j == 0)
    def _():
        n_ref[...] = _rms_bf16(h_ref[...], g_ref[...])

    n = n_ref[...]
    gate_a = jax.nn.sigmoid(_dot(n, wga_ref[...]) + bga_ref[...])
    gate_b = jax.nn.sigmoid(_dot(n, wgb_ref[...]) + bgb_ref[...])
    y_a = _dot(a_ref[...], wa_ref[...])
    y_b = _dot(b_ref[...], wb_ref[...])
    merged = (gate_a * y_a + gate_b * y_b).astype(BF16)
    part = _dot(merged, wo_ref[...])

    @pl.when(j == 0)
    def _():
        o_ref[...] = part

    @pl.when(j > 0)
    def _():
        o_ref[...] += part

    @pl.when(j == pl.num_programs(1) - 1)
    def _():
        o_ref[...] = h_ref[...] + o_ref[...]


def _merge(h, norm_g, a, b, w_in, b_in, w_a, w_b, w_o):
    s = h.shape[0]
    n_c = D_MODEL // COL_TILE
    ga0 = 3 * D_MODEL // COL_TILE
    gb0 = 4 * D_MODEL // COL_TILE
    row = lambda i, j: (i, 0)
    const = lambda i, j: (0, 0)
    colw = pl.BlockSpec((D_MODEL, COL_TILE), lambda i, j: (0, j))
    return pl.pallas_call(
        _merge_kernel,
        grid=(s // ROW_TILE, n_c),
        in_specs=[
            pl.BlockSpec((ROW_TILE, D_MODEL), row),
            pl.BlockSpec((1, D_MODEL), const),
            pl.BlockSpec((ROW_TILE, D_MODEL), row),
            pl.BlockSpec((ROW_TILE, D_MODEL), row),
            pl.BlockSpec((D_MODEL, COL_TILE), lambda i, j: (0, ga0 + j)),
            pl.BlockSpec((D_MODEL, COL_TILE), lambda i, j: (0, gb0 + j)),
            pl.BlockSpec((1, COL_TILE), lambda i, j: (0, ga0 + j)),
            pl.BlockSpec((1, COL_TILE), lambda i, j: (0, gb0 + j)),
            colw,
            colw,
            pl.BlockSpec((COL_TILE, D_MODEL), lambda i, j: (j, 0)),
        ],
        out_specs=pl.BlockSpec((ROW_TILE, D_MODEL), row),
        out_shape=jax.ShapeDtypeStruct((s, D_MODEL), F32),
        scratch_shapes=[pltpu.VMEM((ROW_TILE, D_MODEL), BF16)],
        compiler_params=_params(("arbitrary", "arbitrary")),
        name="merge",
    )(h, norm_g, a, b, w_in, w_in, b_in, b_in, w_a, w_b, w_o)


def kernel(x, ffn1_norm, ffn1_w_in, ffn1_w_out, mix_norm, w_in, b_in, sgu_ln_g, sgu_ln_b, sgu_w_s, sgu_b_s, pool_w, pool_scale, w_branch_a, w_branch_b, w_out, ffn2_norm, ffn2_w_in, ffn2_w_out, final_norm):
    bsz, seq, d = x.shape
    assert d == D_MODEL and seq % ROW_TILE == 0
    row2 = lambda v: v.reshape(1, -1)
    bf = lambda w: w.astype(BF16)

    h = x.reshape(bsz * seq, d)
    h = _ffn(h, row2(ffn1_norm), bf(ffn1_w_in), bf(ffn1_w_out), row2(final_norm), final_norm=False)
    w_in_bf = bf(w_in)
    b_in2 = row2(b_in)
    a = _sgu(h, row2(mix_norm), w_in_bf, b_in2, row2(sgu_ln_g), row2(sgu_ln_b), sgu_w_s, sgu_b_s.T)
    b = _pool(h, row2(mix_norm), w_in_bf, b_in2, bf(pool_w), row2(pool_scale), seq=seq)
    h = _merge(h, row2(mix_norm), a, b, w_in_bf, b_in2, bf(w_branch_a), bf(w_branch_b), bf(w_out))
    h = _ffn(h, row2(ffn2_norm), bf(ffn2_w_in), bf(ffn2_w_out), row2(final_norm), final_norm=True)
    return h.reshape(bsz, seq, d).astype(x.dtype)
```

```python
import functools

import jax
import jax.numpy as jnp
from jax import lax
from jax.experimental import pallas as pl
from jax.experimental.pallas import tpu as pltpu

D_MODEL = 2048
D_FF = 5632
CHUNK = 64
SGU_BLOCK = 128
SGU_HEADS = 8
SGU_HEAD_DIM = D_MODEL // SGU_HEADS
POOL_WINDOWS = (2, 4, 8, 16)
POOL_GROUP_DIM = D_MODEL // len(POOL_WINDOWS)
POOL_HALO = 16
RMS_EPS = 1e-6
LN_EPS = 1e-5

ROW_TILE = 512
FFN_ROW_TILE = 1024
NORM_ROWS = 128
COL_TILE = 512
SGU_ROW_TILE = 512
SGU_COL_TILE = 1024
SGU_STEPS = D_MODEL // SGU_COL_TILE
V7X_VMEM_LIMIT_BYTES = 60 * 1024 * 1024

F32 = jnp.float32
BF16 = jnp.bfloat16


def _rms_bf16(x, g):
    y = x * lax.rsqrt(jnp.mean(x * x, axis=-1, keepdims=True) + RMS_EPS)
    return (y * g).astype(BF16)


def _gelu(x):
    return 0.5 * x * (1.0 + lax.erf(x * (2.0 ** -0.5)))


def _dot(a, b):
    return jnp.dot(a, b, preferred_element_type=F32)


def _params(semantics):
    return pltpu.CompilerParams(dimension_semantics=semantics,
                                vmem_limit_bytes=V7X_VMEM_LIMIT_BYTES)


def _ffn_kernel(x_ref, g_ref, wg_ref, wu_ref, wo_ref, fin_ref, o_ref, xn_ref, *, final_norm):
    j = pl.program_id(1)

    def for_row_blocks(body):
        def step(r, carry):
            body(pl.ds(pl.multiple_of(r * NORM_ROWS, NORM_ROWS), NORM_ROWS))
            return carry
        lax.fori_loop(0, FFN_ROW_TILE // NORM_ROWS, step, None)

    @pl.when(j == 0)
    def _():
        def norm_in(rs):
            xn_ref[rs, :] = _rms_bf16(x_ref[rs, :], g_ref[...])
        for_row_blocks(norm_in)
        o_ref[...] = jnp.zeros_like(o_ref)

    xn = xn_ref[...]
    gate = _dot(xn, wg_ref[...])
    up = _dot(xn, wu_ref[...])
    act = (gate * jax.nn.sigmoid(gate) * up).astype(BF16)
    o_ref[...] += _dot(act, wo_ref[...])

    @pl.when(j == pl.num_programs(1) - 1)
    def _():
        def residual_out(rs):
            h = x_ref[rs, :] + 0.5 * o_ref[rs, :]
            if final_norm:
                h = h * lax.rsqrt(jnp.mean(h * h, axis=-1, keepdims=True) + RMS_EPS) * fin_ref[...]
            o_ref[rs, :] = h
        for_row_blocks(residual_out)


def _ffn(x, norm_g, w_in, w_out, fin_g, *, final_norm):
    s = x.shape[0]
    n_ff = D_FF // COL_TILE
    row = lambda i, j: (i, 0)
    const = lambda i, j: (0, 0)
    return pl.pallas_call(
        functools.partial(_ffn_kernel, final_norm=final_norm),
        grid=(s // FFN_ROW_TILE, n_ff),
        in_specs=[
            pl.BlockSpec((FFN_ROW_TILE, D_MODEL), row),
            pl.BlockSpec((1, D_MODEL), const),
            pl.BlockSpec((D_MODEL, COL_TILE), lambda i, j: (0, j)),
            pl.BlockSpec((D_MODEL, COL_TILE), lambda i, j: (0, j + n_ff)),
            pl.BlockSpec((COL_TILE, D_MODEL), lambda i, j: (j, 0)),
            pl.BlockSpec((1, D_MODEL), const),
        ],
        out_specs=pl.BlockSpec((FFN_ROW_TILE, D_MODEL), row),
        out_shape=jax.ShapeDtypeStruct((s, D_MODEL), F32),
        scratch_shapes=[pltpu.VMEM((FFN_ROW_TILE, D_MODEL), BF16)],
        compiler_params=_params(("arbitrary", "arbitrary")),
        name="ffn_final" if final_norm else "ffn",
    )(x, norm_g, w_in, w_in, w_out, fin_g)


def _sgu_kernel(h_ref, g_ref, w_ref, b_ref, lng_ref, lnb_ref, ws_ref, bs_ref,
                a_ref, n_ref, v_ref, mu_ref, rstd_ref):
    j = pl.program_id(1)

    @pl.when(j == 0)
    def _():
        n_ref[...] = _rms_bf16(h_ref[...], g_ref[...])

    @pl.when(j < SGU_STEPS)
    def _():
        col = pl.multiple_of(j * SGU_COL_TILE, SGU_COL_TILE)
        v_ref[:, pl.ds(col, SGU_COL_TILE)] = _gelu(_dot(n_ref[...], w_ref[...]) + b_ref[...])

    @pl.when(j == SGU_STEPS)
    def _():
        v = v_ref[...]
        mu = jnp.mean(v, axis=-1, keepdims=True)
        mu_ref[...] = mu
        rstd_ref[...] = lax.rsqrt(jnp.mean(jnp.square(v - mu), axis=-1, keepdims=True) + LN_EPS)

    @pl.when(j >= SGU_STEPS)
    def _():
        c = j - SGU_STEPS
        col = pl.multiple_of(c * SGU_COL_TILE, SGU_COL_TILE)
        u = _gelu(_dot(n_ref[...], w_ref[...]) + b_ref[...])
        v = v_ref[:, pl.ds(col, SGU_COL_TILE)]
        vln = ((v - mu_ref[...]) * rstd_ref[...] * lng_ref[:, pl.ds(col, SGU_COL_TILE)]
               + lnb_ref[:, pl.ds(col, SGU_COL_TILE)]).astype(BF16)
        pi = lax.broadcasted_iota(jnp.int32, (SGU_BLOCK, SGU_BLOCK), 0) // CHUNK
        pj = lax.broadcasted_iota(jnp.int32, (SGU_BLOCK, SGU_BLOCK), 1) // CHUNK
        mask = (pj <= pi).astype(F32)
        heads_per_step = SGU_COL_TILE // SGU_HEAD_DIM
        for hh in range(heads_per_step):
            head = c * heads_per_step + hh
            w = (ws_ref[head] * mask).astype(BF16)
            bias = bs_ref[head]
            cs = slice(hh * SGU_HEAD_DIM, (hh + 1) * SGU_HEAD_DIM)
            for blk in range(SGU_ROW_TILE // SGU_BLOCK):
                rs = slice(blk * SGU_BLOCK, (blk + 1) * SGU_BLOCK)
                sp = _dot(w, vln[rs, cs]) + bias
                a_ref[rs, cs] = (u[rs, cs] * sp).astype(BF16)


def _sgu(h, norm_g, w_in, b_in, ln_g, ln_b, w_s, b_s):
    s = h.shape[0]
    row = lambda i, j: (i, 0)
    const = lambda i, j: (0, 0)
    const3 = lambda i, j: (0, 0, 0)
    wcol = lambda i, j: (0, (j + SGU_STEPS) % (2 * SGU_STEPS))
    return pl.pallas_call(
        _sgu_kernel,
        grid=(s // SGU_ROW_TILE, 2 * SGU_STEPS),
        in_specs=[
            pl.BlockSpec((SGU_ROW_TILE, D_MODEL), row),
            pl.BlockSpec((1, D_MODEL), const),
            pl.BlockSpec((D_MODEL, SGU_COL_TILE), wcol),
            pl.BlockSpec((1, SGU_COL_TILE), wcol),
            pl.BlockSpec((1, D_MODEL), const),
            pl.BlockSpec((1, D_MODEL), const),
            pl.BlockSpec((SGU_HEADS, SGU_BLOCK, SGU_BLOCK), const3),
            pl.BlockSpec((SGU_HEADS, SGU_BLOCK, 1), const3),
        ],
        out_specs=pl.BlockSpec((SGU_ROW_TILE, SGU_COL_TILE),
                               lambda i, j: (i, jnp.maximum(j - SGU_STEPS, 0))),
        out_shape=jax.ShapeDtypeStruct((s, D_MODEL), BF16),
        scratch_shapes=[pltpu.VMEM((SGU_ROW_TILE, D_MODEL), BF16),
                        pltpu.VMEM((SGU_ROW_TILE, D_MODEL), F32),
                        pltpu.VMEM((SGU_ROW_TILE, 1), F32),
                        pltpu.VMEM((SGU_ROW_TILE, 1), F32)],
        compiler_params=_params(("arbitrary", "arbitrary")),
        name="sgu",
    )(h, norm_g, w_in, b_in, ln_g, ln_b, w_s, b_s)


def _pool_kernel(h_ref, g_ref, w_ref, b_ref, pw_ref, ps_ref, o_ref, z_ref, *, seq):
    t0 = (pl.program_id(0) * ROW_TILE) % seq

    @pl.when(t0 == 0)
    def _():
        z_ref[0:POOL_HALO, :] = jnp.zeros((POOL_HALO, D_MODEL), F32)

    n = _rms_bf16(h_ref[...], g_ref[...])
    t = t0 + lax.broadcasted_iota(jnp.int32, (ROW_TILE, 1), 0)
    for k, win in reversed(list(enumerate(POOL_WINDOWS))):
        cs = slice(k * POOL_GROUP_DIM, (k + 1) * POOL_GROUP_DIM)
        z = _dot(n, w_ref[:, cs]) + b_ref[:, cs]
        z_ref[POOL_HALO:, cs] = z
        acc = z_ref[:, cs]
        d = 1
        while d < win:
            acc = acc + pltpu.roll(acc, d, axis=0)
            d *= 2
        inv_cnt = 1.0 / jnp.minimum(t + 1, win).astype(F32)
        pooled = (acc[POOL_HALO:] * inv_cnt - z).astype(BF16)
        mixed = _dot(pooled, pw_ref[k])
        o_ref[:, cs] = (mixed * ps_ref[:, cs]).astype(BF16)
    z_ref[0:POOL_HALO, :] = z_ref[ROW_TILE:ROW_TILE + POOL_HALO, :]


def _pool(h, norm_g, w_in, b_in, pool_w, pool_scale, *, seq):
    s = h.shape[0]
    zb_block = 2 * D_MODEL // D_MODEL
    return pl.pallas_call(
        functools.partial(_pool_kernel, seq=seq),
        grid=(s // ROW_TILE,),
        in_specs=[
            pl.BlockSpec((ROW_TILE, D_MODEL), lambda i: (i, 0)),
            pl.BlockSpec((1, D_MODEL), lambda i: (0, 0)),
            pl.BlockSpec((D_MODEL, D_MODEL), lambda i: (0, zb_block)),
            pl.BlockSpec((1, D_MODEL), lambda i: (0, zb_block)),
            pl.BlockSpec((len(POOL_WINDOWS), POOL_GROUP_DIM, POOL_GROUP_DIM), lambda i: (0, 0, 0)),
            pl.BlockSpec((1, D_MODEL), lambda i: (0, 0)),
        ],
        out_specs=pl.BlockSpec((ROW_TILE, D_MODEL), lambda i: (i, 0)),
        out_shape=jax.ShapeDtypeStruct((s, D_MODEL), BF16),
        scratch_shapes=[pltpu.VMEM((ROW_TILE + POOL_HALO, D_MODEL), F32)],
        compiler_params=_params(("arbitrary",)),
        name="pool",
    )(h, norm_g, w_in, b_in, pool_w, pool_scale)


def _merge_kernel(h_ref, g_ref, a_ref, b_ref, wga_ref, wgb_ref, bga_ref, bgb_ref,
                  wa_ref, wb_ref, wo_ref, o_ref, n_ref):
    j = pl.program_id(1)

    @pl.when(j == 0)
    def _():
        n_ref[...] = _rms_bf16(h_ref[...], g_ref[...])
        o_ref[...] = jnp.zeros_like(o_ref)

    n = n_ref[...]
    gate_a = jax.nn.sigmoid(_dot(n, wga_ref[...]) + bga_ref[...])
    gate_b = jax.nn.sigmoid(_dot(n, wgb_ref[...]) + bgb_ref[...])
    y_a = _dot(a_ref[...], wa_ref[...])
    y_b = _dot(b_ref[...], wb_ref[...])
    merged = (gate_a * y_a + gate_b * y_b).astype(BF16)
    o_ref[...] += _dot(merged, wo_ref[...])

    @pl.when(j == pl.num_programs(1) - 1)
    def _():
        o_ref[...] = h_ref[...] + o_ref[...]


def _merge(h, norm_g, a, b, w_in, b_in, w_a, w_b, w_o):
    s = h.shape[0]
    n_c = D_MODEL // COL_TILE
    ga0 = 3 * D_MODEL // COL_TILE
    gb0 = 4 * D_MODEL // COL_TILE
    row = lambda i, j: (i, 0)
    const = lambda i, j: (0, 0)
    colw = pl.BlockSpec((D_MODEL, COL_TILE), lambda i, j: (0, j))
    return pl.pallas_call(
        _merge_kernel,
        grid=(s // ROW_TILE, n_c),
        in_specs=[
            pl.BlockSpec((ROW_TILE, D_MODEL), row),
            pl.BlockSpec((1, D_MODEL), const),
            pl.BlockSpec((ROW_TILE, D_MODEL), row),
            pl.BlockSpec((ROW_TILE, D_MODEL), row),
            pl.BlockSpec((D_MODEL, COL_TILE), lambda i, j: (0, ga0 + j)),
            pl.BlockSpec((D_MODEL, COL_TILE), lambda i, j: (0, gb0 + j)),
            pl.BlockSpec((1, COL_TILE), lambda i, j: (0, ga0 + j)),
            pl.BlockSpec((1, COL_TILE), lambda i, j: (0, gb0 + j)),
            colw,
            colw,
            pl.BlockSpec((COL_TILE, D_MODEL), lambda i, j: (j, 0)),
        ],
        out_specs=pl.BlockSpec((ROW_TILE, D_MODEL), row),
        out_shape=jax.ShapeDtypeStruct((s, D_MODEL), F32),
        scratch_shapes=[pltpu.VMEM((ROW_TILE, D_MODEL), BF16)],
        compiler_params=_params(("arbitrary", "arbitrary")),
        name="merge",
    )(h, norm_g, a, b, w_in, w_in, b_in, b_in, w_a, w_b, w_o)


def kernel(x, ffn1_norm, ffn1_w_in, ffn1_w_out, mix_norm, w_in, b_in, sgu_ln_g, sgu_ln_b, sgu_w_s, sgu_b_s, pool_w, pool_scale, w_branch_a, w_branch_b, w_out, ffn2_norm, ffn2_w_in, ffn2_w_out, final_norm):
    bsz, seq, d = x.shape
    assert d == D_MODEL and seq % ROW_TILE == 0 and (bsz * seq) % FFN_ROW_TILE == 0
    row2 = lambda v: v.reshape(1, -1)
    bf = lambda w: w.astype(BF16)

    h = x.reshape(bsz * seq, d)
    h = _ffn(h, row2(ffn1_norm), bf(ffn1_w_in), bf(ffn1_w_out), row2(final_norm), final_norm=False)
    w_in_bf = bf(w_in)
    b_in2 = row2(b_in)
    a = _sgu(h, row2(mix_norm), w_in_bf, b_in2, row2(sgu_ln_g), row2(sgu_ln_b), sgu_w_s, sgu_b_s[:, :, None])
    b = _pool(h, row2(mix_norm), w_in_bf, b_in2, bf(pool_w), row2(pool_scale), seq=seq)
    h = _merge(h, row2(mix_norm), a, b, w_in_bf, b_in2, bf(w_branch_a), bf(w_branch_b), bf(w_out))
    h = _ffn(h, row2(ffn2_norm), bf(ffn2_w_in), bf(ffn2_w_out), row2(final_norm), final_norm=True)
    return h.reshape(bsz, seq, d).astype(x.dtype)
```

```python
import functools

import jax
import jax.numpy as jnp
from jax import lax
from jax.experimental import pallas as pl
from jax.experimental.pallas import tpu as pltpu

D_MODEL = 2048
D_FF = 5632
CHUNK = 64
SGU_BLOCK = 128
SGU_HEADS = 8
SGU_HEAD_DIM = D_MODEL // SGU_HEADS
POOL_WINDOWS = (2, 4, 8, 16)
POOL_GROUP_DIM = D_MODEL // len(POOL_WINDOWS)
POOL_HALO = 16
RMS_EPS = 1e-6
LN_EPS = 1e-5

ROW_TILE = 512
FFN_ROW_TILE = 1024
NORM_ROWS = 128
X_CHUNKS = 4
X_CHUNK = D_MODEL // X_CHUNKS
COL_TILE = 512
SGU_ROW_TILE = 512
SGU_COL_TILE = 1024
SGU_STEPS = D_MODEL // SGU_COL_TILE
V7X_VMEM_LIMIT_BYTES = 60 * 1024 * 1024

F32 = jnp.float32
BF16 = jnp.bfloat16


def _rms_bf16(x, g):
    y = x * lax.rsqrt(jnp.mean(x * x, axis=-1, keepdims=True) + RMS_EPS)
    return (y * g).astype(BF16)


def _staggered_row_specs(row_tile, n_tiles, n_steps):
    specs = []
    for k in range(X_CHUNKS):
        switch = k * n_steps // X_CHUNKS

        def index(i, j, k=k, switch=switch):
            return (jnp.minimum(i + jnp.where(j > switch, 1, 0), n_tiles - 1), k)
        specs.append(pl.BlockSpec((row_tile, X_CHUNK), index))
    return specs


def _norm_tile(x_refs, g_ref, n_ref, rows, copy_ref=None):
    def step(r, carry):
        rs = pl.ds(pl.multiple_of(r * NORM_ROWS, NORM_ROWS), NORM_ROWS)
        xs = [x[rs, :] for x in x_refs]
        ss = jnp.sum(xs[0] * xs[0], axis=-1, keepdims=True)
        for x in xs[1:]:
            ss = ss + jnp.sum(x * x, axis=-1, keepdims=True)
        rstd = lax.rsqrt(ss * (1.0 / D_MODEL) + RMS_EPS)
        for k, x in enumerate(xs):
            cs = slice(k * X_CHUNK, (k + 1) * X_CHUNK)
            n_ref[rs, cs] = (x * rstd * g_ref[:, cs]).astype(BF16)
            if copy_ref is not None:
                copy_ref[rs, cs] = x
        return carry
    lax.fori_loop(0, rows // NORM_ROWS, step, None)


def _gelu(x):
    return 0.5 * x * (1.0 + lax.erf(x * (2.0 ** -0.5)))


def _dot(a, b):
    return jnp.dot(a, b, preferred_element_type=F32)


def _params(semantics):
    return pltpu.CompilerParams(dimension_semantics=semantics,
                                vmem_limit_bytes=V7X_VMEM_LIMIT_BYTES)


def _ffn_kernel(*refs, final_norm):
    x_refs = refs[:X_CHUNKS]
    g_ref, wg_ref, wu_ref, wo_ref, fin_ref, o_ref, xn_ref = refs[X_CHUNKS:]
    j = pl.program_id(1)

    @pl.when(j == 0)
    def _():
        _norm_tile(x_refs, g_ref, xn_ref, FFN_ROW_TILE, copy_ref=o_ref)

    xn = xn_ref[...]
    gate = _dot(xn, wg_ref[...])
    up = _dot(xn, wu_ref[...])
    act = (0.5 * gate * jax.nn.sigmoid(gate) * up).astype(BF16)
    o_ref[...] += _dot(act, wo_ref[...])

    if final_norm:
        @pl.when(j == pl.num_programs(1) - 1)
        def _():
            def step(r, carry):
                rs = pl.ds(pl.multiple_of(r * NORM_ROWS, NORM_ROWS), NORM_ROWS)
                h = o_ref[rs, :]
                o_ref[rs, :] = h * lax.rsqrt(jnp.mean(h * h, axis=-1, keepdims=True) + RMS_EPS) * fin_ref[...]
                return carry
            lax.fori_loop(0, FFN_ROW_TILE // NORM_ROWS, step, None)


def _ffn(x, norm_g, w_in, w_out, fin_g, *, final_norm):
    s = x.shape[0]
    n_ff = D_FF // COL_TILE
    n_tiles = s // FFN_ROW_TILE
    row = lambda i, j: (i, 0)
    const = lambda i, j: (0, 0)
    return pl.pallas_call(
        functools.partial(_ffn_kernel, final_norm=final_norm),
        grid=(n_tiles, n_ff),
        in_specs=_staggered_row_specs(FFN_ROW_TILE, n_tiles, n_ff) + [
            pl.BlockSpec((1, D_MODEL), const),
            pl.BlockSpec((D_MODEL, COL_TILE), lambda i, j: (0, j)),
            pl.BlockSpec((D_MODEL, COL_TILE), lambda i, j: (0, j + n_ff)),
            pl.BlockSpec((COL_TILE, D_MODEL), lambda i, j: (j, 0)),
            pl.BlockSpec((1, D_MODEL), const),
        ],
        out_specs=pl.BlockSpec((FFN_ROW_TILE, D_MODEL), row),
        out_shape=jax.ShapeDtypeStruct((s, D_MODEL), F32),
        scratch_shapes=[pltpu.VMEM((FFN_ROW_TILE, D_MODEL), BF16)],
        compiler_params=_params(("arbitrary", "arbitrary")),
        name="ffn_final" if final_norm else "ffn",
    )(*([x] * X_CHUNKS), norm_g, w_in, w_in, w_out, fin_g)


def _sgu_kernel(*refs):
    h_refs = refs[:X_CHUNKS]
    (g_ref, w_ref, b_ref, lng_ref, lnb_ref, ws_ref, bs_ref,
     a_ref, n_ref, v_ref, mu_ref, rstd_ref) = refs[X_CHUNKS:]
    j = pl.program_id(1)

    @pl.when(j == 0)
    def _():
        _norm_tile(h_refs, g_ref, n_ref, SGU_ROW_TILE)

    @pl.when(j < SGU_STEPS)
    def _():
        col = pl.multiple_of(j * SGU_COL_TILE, SGU_COL_TILE)
        v_ref[:, pl.ds(col, SGU_COL_TILE)] = _gelu(_dot(n_ref[...], w_ref[...]) + b_ref[...])

    @pl.when(j == SGU_STEPS)
    def _():
        v = v_ref[...]
        mu = jnp.mean(v, axis=-1, keepdims=True)
        mu_ref[...] = mu
        rstd_ref[...] = lax.rsqrt(jnp.mean(jnp.square(v - mu), axis=-1, keepdims=True) + LN_EPS)

    @pl.when(j >= SGU_STEPS)
    def _():
        c = j - SGU_STEPS
        col = pl.multiple_of(c * SGU_COL_TILE, SGU_COL_TILE)
        u = _gelu(_dot(n_ref[...], w_ref[...]) + b_ref[...])
        v = v_ref[:, pl.ds(col, SGU_COL_TILE)]
        vln = ((v - mu_ref[...]) * rstd_ref[...] * lng_ref[:, pl.ds(col, SGU_COL_TILE)]
               + lnb_ref[:, pl.ds(col, SGU_COL_TILE)]).astype(BF16)
        pi = lax.broadcasted_iota(jnp.int32, (SGU_BLOCK, SGU_BLOCK), 0) // CHUNK
        pj = lax.broadcasted_iota(jnp.int32, (SGU_BLOCK, SGU_BLOCK), 1) // CHUNK
        mask = (pj <= pi).astype(F32)
        heads_per_step = SGU_COL_TILE // SGU_HEAD_DIM
        for hh in range(heads_per_step):
            head = c * heads_per_step + hh
            w = (ws_ref[head] * mask).astype(BF16)
            bias = bs_ref[head]
            cs = slice(hh * SGU_HEAD_DIM, (hh + 1) * SGU_HEAD_DIM)
            for blk in range(SGU_ROW_TILE // SGU_BLOCK):
                rs = slice(blk * SGU_BLOCK, (blk + 1) * SGU_BLOCK)
                sp = _dot(w, vln[rs, cs]) + bias
                a_ref[rs, cs] = (u[rs, cs] * sp).astype(BF16)


def _sgu(h, norm_g, w_in, b_in, ln_g, ln_b, w_s, b_s):
    s = h.shape[0]
    const = lambda i, j: (0, 0)
    const3 = lambda i, j: (0, 0, 0)
    wcol = lambda i, j: (0, (j + SGU_STEPS) % (2 * SGU_STEPS))
    n_tiles = s // SGU_ROW_TILE
    return pl.pallas_call(
        _sgu_kernel,
        grid=(n_tiles, 2 * SGU_STEPS),
        in_specs=_staggered_row_specs(SGU_ROW_TILE, n_tiles, 2 * SGU_STEPS) + [
            pl.BlockSpec((1, D_MODEL), const),
            pl.BlockSpec((D_MODEL, SGU_COL_TILE), wcol),
            pl.BlockSpec((1, SGU_COL_TILE), wcol),
            pl.BlockSpec((1, D_MODEL), const),
            pl.BlockSpec((1, D_MODEL), const),
            pl.BlockSpec((SGU_HEADS, SGU_BLOCK, SGU_BLOCK), const3),
            pl.BlockSpec((SGU_HEADS, SGU_BLOCK, 1), const3),
        ],
        out_specs=pl.BlockSpec((SGU_ROW_TILE, SGU_COL_TILE),
                               lambda i, j: (i, jnp.maximum(j - SGU_STEPS, 0))),
        out_shape=jax.ShapeDtypeStruct((s, D_MODEL), BF16),
        scratch_shapes=[pltpu.VMEM((SGU_ROW_TILE, D_MODEL), BF16),
                        pltpu.VMEM((SGU_ROW_TILE, D_MODEL), F32),
                        pltpu.VMEM((SGU_ROW_TILE, 1), F32),
                        pltpu.VMEM((SGU_ROW_TILE, 1), F32)],
        compiler_params=_params(("arbitrary", "arbitrary")),
        name="sgu",
    )(*([h] * X_CHUNKS), norm_g, w_in, b_in, ln_g, ln_b, w_s, b_s)


def _pool_kernel(h_ref, g_ref, w_ref, b_ref, pw_ref, ps_ref, o_ref, z_ref, *, seq):
    t0 = (pl.program_id(0) * ROW_TILE) % seq

    @pl.when(t0 == 0)
    def _():
        z_ref[0:POOL_HALO, :] = jnp.zeros((POOL_HALO, D_MODEL), F32)

    n = _rms_bf16(h_ref[...], g_ref[...])
    t = t0 + lax.broadcasted_iota(jnp.int32, (ROW_TILE, 1), 0)
    for k, win in reversed(list(enumerate(POOL_WINDOWS))):
        cs = slice(k * POOL_GROUP_DIM, (k + 1) * POOL_GROUP_DIM)
        z = _dot(n, w_ref[:, cs]) + b_ref[:, cs]
        z_ref[POOL_HALO:, cs] = z
        acc = z_ref[:, cs]
        d = 1
        while d < win:
            acc = acc + pltpu.roll(acc, d, axis=0)
            d *= 2
        inv_cnt = 1.0 / jnp.minimum(t + 1, win).astype(F32)
        pooled = (acc[POOL_HALO:] * inv_cnt - z).astype(BF16)
        mixed = _dot(pooled, pw_ref[k])
        o_ref[:, cs] = (mixed * ps_ref[:, cs]).astype(BF16)
    z_ref[0:POOL_HALO, :] = z_ref[ROW_TILE:ROW_TILE + POOL_HALO, :]


def _pool(h, norm_g, w_in, b_in, pool_w, pool_scale, *, seq):
    s = h.shape[0]
    zb_block = 2 * D_MODEL // D_MODEL
    return pl.pallas_call(
        functools.partial(_pool_kernel, seq=seq),
        grid=(s // ROW_TILE,),
        in_specs=[
            pl.BlockSpec((ROW_TILE, D_MODEL), lambda i: (i, 0)),
            pl.BlockSpec((1, D_MODEL), lambda i: (0, 0)),
            pl.BlockSpec((D_MODEL, D_MODEL), lambda i: (0, zb_block)),
            pl.BlockSpec((1, D_MODEL), lambda i: (0, zb_block)),
            pl.BlockSpec((len(POOL_WINDOWS), POOL_GROUP_DIM, POOL_GROUP_DIM), lambda i: (0, 0, 0)),
            pl.BlockSpec((1, D_MODEL), lambda i: (0, 0)),
        ],
        out_specs=pl.BlockSpec((ROW_TILE, D_MODEL), lambda i: (i, 0)),
        out_shape=jax.ShapeDtypeStruct((s, D_MODEL), BF16),
        scratch_shapes=[pltpu.VMEM((ROW_TILE + POOL_HALO, D_MODEL), F32)],
        compiler_params=_params(("arbitrary",)),
        name="pool",
    )(h, norm_g, w_in, b_in, pool_w, pool_scale)


def _merge_kernel(*refs):
    h_refs = refs[:X_CHUNKS]
    (g_ref, a_ref, b_ref, wga_ref, wgb_ref, bga_ref, bgb_ref,
     wa_ref, wb_ref, wo_ref, o_ref, n_ref) = refs[X_CHUNKS:]
    j = pl.program_id(1)

    @pl.when(j == 0)
    def _():
        _norm_tile(h_refs, g_ref, n_ref, ROW_TILE, copy_ref=o_ref)

    n = n_ref[...]
    gate_a = jax.nn.sigmoid(_dot(n, wga_ref[...]) + bga_ref[...])
    gate_b = jax.nn.sigmoid(_dot(n, wgb_ref[...]) + bgb_ref[...])
    y_a = _dot(a_ref[...], wa_ref[...])
    y_b = _dot(b_ref[...], wb_ref[...])
    merged = (gate_a * y_a + gate_b * y_b).astype(BF16)
    o_ref[...] += _dot(merged, wo_ref[...])


def _merge(h, norm_g, a, b, w_in, b_in, w_a, w_b, w_o):
    s = h.shape[0]
    n_c = D_MODEL // COL_TILE
    ga0 = 3 * D_MODEL // COL_TILE
    gb0 = 4 * D_MODEL // COL_TILE
    row = lambda i, j: (i, 0)
    const = lambda i, j: (0, 0)
    colw = pl.BlockSpec((D_MODEL, COL_TILE), lambda i, j: (0, j))
    n_tiles = s // ROW_TILE
    return pl.pallas_call(
        _merge_kernel,
        grid=(n_tiles, n_c),
        in_specs=_staggered_row_specs(ROW_TILE, n_tiles, n_c) + [
            pl.BlockSpec((1, D_MODEL), const),
            pl.BlockSpec((ROW_TILE, D_MODEL), row),
            pl.BlockSpec((ROW_TILE, D_MODEL), row),
            pl.BlockSpec((D_MODEL, COL_TILE), lambda i, j: (0, ga0 + j)),
            pl.BlockSpec((D_MODEL, COL_TILE), lambda i, j: (0, gb0 + j)),
            pl.BlockSpec((1, COL_TILE), lambda i, j: (0, ga0 + j)),
            pl.BlockSpec((1, COL_TILE), lambda i, j: (0, gb0 + j)),
            colw,
            colw,
            pl.BlockSpec((COL_TILE, D_MODEL), lambda i, j: (j, 0)),
        ],
        out_specs=pl.BlockSpec((ROW_TILE, D_MODEL), row),
        out_shape=jax.ShapeDtypeStruct((s, D_MODEL), F32),
        scratch_shapes=[pltpu.VMEM((ROW_TILE, D_MODEL), BF16)],
        compiler_params=_params(("arbitrary", "arbitrary")),
        name="merge",
    )(*([h] * X_CHUNKS), norm_g, a, b, w_in, w_in, b_in, b_in, w_a, w_b, w_o)


def kernel(x, ffn1_norm, ffn1_w_in, ffn1_w_out, mix_norm, w_in, b_in, sgu_ln_g, sgu_ln_b, sgu_w_s, sgu_b_s, pool_w, pool_scale, w_branch_a, w_branch_b, w_out, ffn2_norm, ffn2_w_in, ffn2_w_out, final_norm):
    bsz, seq, d = x.shape
    assert d == D_MODEL and seq % ROW_TILE == 0 and (bsz * seq) % FFN_ROW_TILE == 0
    row2 = lambda v: v.reshape(1, -1)
    bf = lambda w: w.astype(BF16)

    h = x.reshape(bsz * seq, d)
    h = _ffn(h, row2(ffn1_norm), bf(ffn1_w_in), bf(ffn1_w_out), row2(final_norm), final_norm=False)
    w_in_bf = bf(w_in)
    b_in2 = row2(b_in)
    a = _sgu(h, row2(mix_norm), w_in_bf, b_in2, row2(sgu_ln_g), row2(sgu_ln_b), sgu_w_s, sgu_b_s[:, :, None])
    b = _pool(h, row2(mix_norm), w_in_bf, b_in2, bf(pool_w), row2(pool_scale), seq=seq)
    h = _merge(h, row2(mix_norm), a, b, w_in_bf, b_in2, bf(w_branch_a), bf(w_branch_b), bf(w_out))
    h = _ffn(h, row2(ffn2_norm), bf(ffn2_w_in), bf(ffn2_w_out), row2(final_norm), final_norm=True)
    return h.reshape(bsz, seq, d).astype(x.dtype)
```

```python
import functools

import jax
import jax.numpy as jnp
from jax import lax
from jax.experimental import pallas as pl
from jax.experimental.pallas import tpu as pltpu

D_MODEL = 2048
D_FF = 5632
CHUNK = 64
SGU_BLOCK = 128
SGU_HEADS = 8
SGU_HEAD_DIM = D_MODEL // SGU_HEADS
POOL_WINDOWS = (2, 4, 8, 16)
POOL_GROUP_DIM = D_MODEL // len(POOL_WINDOWS)
POOL_HALO = 16
RMS_EPS = 1e-6
LN_EPS = 1e-5

ROW_TILE = 512
FFN_ROW_TILE = 1024
NORM_ROWS = 128
X_CHUNKS = 4
X_CHUNK = D_MODEL // X_CHUNKS
COL_TILE = 512
SGU_ROW_TILE = 512
SGU_COL_TILE = 1024
SGU_STEPS = D_MODEL // SGU_COL_TILE
V7X_VMEM_LIMIT_BYTES = 60 * 1024 * 1024

F32 = jnp.float32
BF16 = jnp.bfloat16
LANES = 128
BF16_SUBLANES = 16


def _rms_bf16(x, g):
    y = x * lax.rsqrt(jnp.mean(x * x, axis=-1, keepdims=True) + RMS_EPS)
    return (y * g).astype(BF16)


def _staggered_row_specs(row_tile, n_tiles, n_steps):
    specs = []
    for k in range(X_CHUNKS):
        switch = k * n_steps // X_CHUNKS

        def index(i, j, k=k, switch=switch):
            return (jnp.minimum(i + jnp.where(j > switch, 1, 0), n_tiles - 1), k)
        specs.append(pl.BlockSpec((row_tile, X_CHUNK), index))
    return specs


def _side_cast_specs(shape, n_tiles, n_steps):
    rows, cols = shape
    assert rows % n_tiles == 0 and (rows // n_tiles) % BF16_SUBLANES == 0
    n_col = max(c for c in range(1, n_steps + 1) if cols % c == 0 and (cols // c) % LANES == 0)
    index = lambda i, j: (i, jnp.minimum(j, n_col - 1))
    return pl.BlockSpec((rows // n_tiles, cols // n_col), index)


def _norm_tile(x_refs, g_ref, n_ref, rows, copy_ref=None):
    def step(r, carry):
        rs = pl.ds(pl.multiple_of(r * NORM_ROWS, NORM_ROWS), NORM_ROWS)
        xs = [x[rs, :] for x in x_refs]
        ss = jnp.sum(xs[0] * xs[0], axis=-1, keepdims=True)
        for x in xs[1:]:
            ss = ss + jnp.sum(x * x, axis=-1, keepdims=True)
        rstd = lax.rsqrt(ss * (1.0 / D_MODEL) + RMS_EPS)
        for k, x in enumerate(xs):
            cs = slice(k * X_CHUNK, (k + 1) * X_CHUNK)
            n_ref[rs, cs] = (x * rstd * g_ref[:, cs]).astype(BF16)
            if copy_ref is not None:
                copy_ref[rs, cs] = x
        return carry
    lax.fori_loop(0, rows // NORM_ROWS, step, None)


def _gelu(x):
    return 0.5 * x * (1.0 + lax.erf(x * (2.0 ** -0.5)))


def _dot(a, b):
    return jnp.dot(a, b, preferred_element_type=F32)


def _params(semantics):
    return pltpu.CompilerParams(dimension_semantics=semantics,
                                vmem_limit_bytes=V7X_VMEM_LIMIT_BYTES)


def _ffn_kernel(*refs, final_norm, n_side):
    x_refs = refs[:X_CHUNKS]
    g_ref, wg_ref, wu_ref, wo_ref, fin_ref = refs[X_CHUNKS:X_CHUNKS + 5]
    side_in = refs[X_CHUNKS + 5:X_CHUNKS + 5 + n_side]
    o_ref = refs[X_CHUNKS + 5 + n_side]
    side_out = refs[X_CHUNKS + 6 + n_side:X_CHUNKS + 6 + 2 * n_side]
    xn_ref = refs[-1]
    j = pl.program_id(1)

    @pl.when(j == 0)
    def _():
        _norm_tile(x_refs, g_ref, xn_ref, FFN_ROW_TILE, copy_ref=o_ref)

    for src, dst in zip(side_in, side_out):
        dst[...] = src[...].astype(BF16)

    xn = xn_ref[...]
    gate = _dot(xn, wg_ref[...])
    up = _dot(xn, wu_ref[...])
    act = (0.5 * gate * jax.nn.sigmoid(gate) * up).astype(BF16)
    o_ref[...] += _dot(act, wo_ref[...])

    if final_norm:
        @pl.when(j == pl.num_programs(1) - 1)
        def _():
            def step(r, carry):
                rs = pl.ds(pl.multiple_of(r * NORM_ROWS, NORM_ROWS), NORM_ROWS)
                h = o_ref[rs, :]
                o_ref[rs, :] = h * lax.rsqrt(jnp.mean(h * h, axis=-1, keepdims=True) + RMS_EPS) * fin_ref[...]
                return carry
            lax.fori_loop(0, FFN_ROW_TILE // NORM_ROWS, step, None)


def _ffn(x, norm_g, w_in, w_out, fin_g, *, final_norm, cast_on_side=()):
    s = x.shape[0]
    n_ff = D_FF // COL_TILE
    n_tiles = s // FFN_ROW_TILE
    row = lambda i, j: (i, 0)
    const = lambda i, j: (0, 0)
    side_specs = [_side_cast_specs(w.shape, n_tiles, n_ff) for w in cast_on_side]
    return pl.pallas_call(
        functools.partial(_ffn_kernel, final_norm=final_norm, n_side=len(cast_on_side)),
        grid=(n_tiles, n_ff),
        in_specs=_staggered_row_specs(FFN_ROW_TILE, n_tiles, n_ff) + [
            pl.BlockSpec((1, D_MODEL), const),
            pl.BlockSpec((D_MODEL, COL_TILE), lambda i, j: (0, j)),
            pl.BlockSpec((D_MODEL, COL_TILE), lambda i, j: (0, j + n_ff)),
            pl.BlockSpec((COL_TILE, D_MODEL), lambda i, j: (j, 0)),
            pl.BlockSpec((1, D_MODEL), const),
        ] + side_specs,
        out_specs=[pl.BlockSpec((FFN_ROW_TILE, D_MODEL), row)] + side_specs,
        out_shape=[jax.ShapeDtypeStruct((s, D_MODEL), F32)]
                  + [jax.ShapeDtypeStruct(w.shape, BF16) for w in cast_on_side],
        scratch_shapes=[pltpu.VMEM((FFN_ROW_TILE, D_MODEL), BF16)],
        compiler_params=_params(("arbitrary", "arbitrary")),
        name="ffn_final" if final_norm else "ffn",
    )(*([x] * X_CHUNKS), norm_g, w_in, w_in, w_out, fin_g, *cast_on_side)


def _sgu_kernel(*refs):
    h_refs = refs[:X_CHUNKS]
    (g_ref, w_ref, b_ref, lng_ref, lnb_ref, ws_ref, bs_ref,
     a_ref, n_ref, v_ref, mu_ref, rstd_ref) = refs[X_CHUNKS:]
    j = pl.program_id(1)

    @pl.when(j == 0)
    def _():
        _norm_tile(h_refs, g_ref, n_ref, SGU_ROW_TILE)

    @pl.when(j < SGU_STEPS)
    def _():
        col = pl.multiple_of(j * SGU_COL_TILE, SGU_COL_TILE)
        v_ref[:, pl.ds(col, SGU_COL_TILE)] = _gelu(_dot(n_ref[...], w_ref[...]) + b_ref[...])

    @pl.when(j == SGU_STEPS)
    def _():
        v = v_ref[...]
        mu = jnp.mean(v, axis=-1, keepdims=True)
        mu_ref[...] = mu
        rstd_ref[...] = lax.rsqrt(jnp.mean(jnp.square(v - mu), axis=-1, keepdims=True) + LN_EPS)

    @pl.when(j >= SGU_STEPS)
    def _():
        c = j - SGU_STEPS
        col = pl.multiple_of(c * SGU_COL_TILE, SGU_COL_TILE)
        u = _gelu(_dot(n_ref[...], w_ref[...]) + b_ref[...])
        v = v_ref[:, pl.ds(col, SGU_COL_TILE)]
        vln = ((v - mu_ref[...]) * rstd_ref[...] * lng_ref[:, pl.ds(col, SGU_COL_TILE)]
               + lnb_ref[:, pl.ds(col, SGU_COL_TILE)]).astype(BF16)
        pi = lax.broadcasted_iota(jnp.int32, (SGU_BLOCK, SGU_BLOCK), 0) // CHUNK
        pj = lax.broadcasted_iota(jnp.int32, (SGU_BLOCK, SGU_BLOCK), 1) // CHUNK
        mask = (pj <= pi).astype(F32)
        heads_per_step = SGU_COL_TILE // SGU_HEAD_DIM
        for hh in range(heads_per_step):
            head = c * heads_per_step + hh
            w = (ws_ref[head] * mask).astype(BF16)
            bias = bs_ref[head]
            cs = slice(hh * SGU_HEAD_DIM, (hh + 1) * SGU_HEAD_DIM)
            for blk in range(SGU_ROW_TILE // SGU_BLOCK):
                rs = slice(blk * SGU_BLOCK, (blk + 1) * SGU_BLOCK)
                sp = _dot(w, vln[rs, cs]) + bias
                a_ref[rs, cs] = (u[rs, cs] * sp).astype(BF16)


def _sgu(h, norm_g, w_in, b_in, ln_g, ln_b, w_s, b_s):
    s = h.shape[0]
    const = lambda i, j: (0, 0)
    const3 = lambda i, j: (0, 0, 0)
    wcol = lambda i, j: (0, (j + SGU_STEPS) % (2 * SGU_STEPS))
    n_tiles = s // SGU_ROW_TILE
    return pl.pallas_call(
        _sgu_kernel,
        grid=(n_tiles, 2 * SGU_STEPS),
        in_specs=_staggered_row_specs(SGU_ROW_TILE, n_tiles, 2 * SGU_STEPS) + [
            pl.BlockSpec((1, D_MODEL), const),
            pl.BlockSpec((D_MODEL, SGU_COL_TILE), wcol),
            pl.BlockSpec((1, SGU_COL_TILE), wcol),
            pl.BlockSpec((1, D_MODEL), const),
            pl.BlockSpec((1, D_MODEL), const),
            pl.BlockSpec((SGU_HEADS, SGU_BLOCK, SGU_BLOCK), const3),
            pl.BlockSpec((SGU_HEADS, SGU_BLOCK, 1), const3),
        ],
        out_specs=pl.BlockSpec((SGU_ROW_TILE, SGU_COL_TILE),
                               lambda i, j: (i, jnp.maximum(j - SGU_STEPS, 0))),
        out_shape=jax.ShapeDtypeStruct((s, D_MODEL), BF16),
        scratch_shapes=[pltpu.VMEM((SGU_ROW_TILE, D_MODEL), BF16),
                        pltpu.VMEM((SGU_ROW_TILE, D_MODEL), F32),
                        pltpu.VMEM((SGU_ROW_TILE, 1), F32),
                        pltpu.VMEM((SGU_ROW_TILE, 1), F32)],
        compiler_params=_params(("arbitrary", "arbitrary")),
        name="sgu",
    )(*([h] * X_CHUNKS), norm_g, w_in, b_in, ln_g, ln_b, w_s, b_s)


def _pool_kernel(h_ref, g_ref, w_ref, b_ref, pw_ref, ps_ref, o_ref, z_ref, *, seq):
    t0 = (pl.program_id(0) * ROW_TILE) % seq

    @pl.when(t0 == 0)
    def _():
        z_ref[0:POOL_HALO, :] = jnp.zeros((POOL_HALO, D_MODEL), F32)

    n = _rms_bf16(h_ref[...], g_ref[...])
    t = t0 + lax.broadcasted_iota(jnp.int32, (ROW_TILE, 1), 0)
    for k, win in reversed(list(enumerate(POOL_WINDOWS))):
        cs = slice(k * POOL_GROUP_DIM, (k + 1) * POOL_GROUP_DIM)
        z = _dot(n, w_ref[:, cs]) + b_ref[:, cs]
        z_ref[POOL_HALO:, cs] = z
        acc = z_ref[:, cs]
        d = 1
        while d < win:
            acc = acc + pltpu.roll(acc, d, axis=0)
            d *= 2
        inv_cnt = 1.0 / jnp.minimum(t + 1, win).astype(F32)
        pooled = (acc[POOL_HALO:] * inv_cnt - z).astype(BF16)
        mixed = _dot(pooled, pw_ref[k])
        o_ref[:, cs] = (mixed * ps_ref[:, cs]).astype(BF16)
    z_ref[0:POOL_HALO, :] = z_ref[ROW_TILE:ROW_TILE + POOL_HALO, :]


def _pool(h, norm_g, w_in, b_in, pool_w, pool_scale, *, seq):
    s = h.shape[0]
    zb_block = 2 * D_MODEL // D_MODEL
    return pl.pallas_call(
        functools.partial(_pool_kernel, seq=seq),
        grid=(s // ROW_TILE,),
        in_specs=[
            pl.BlockSpec((ROW_TILE, D_MODEL), lambda i: (i, 0)),
            pl.BlockSpec((1, D_MODEL), lambda i: (0, 0)),
            pl.BlockSpec((D_MODEL, D_MODEL), lambda i: (0, zb_block)),
            pl.BlockSpec((1, D_MODEL), lambda i: (0, zb_block)),
            pl.BlockSpec((len(POOL_WINDOWS), POOL_GROUP_DIM, POOL_GROUP_DIM), lambda i: (0, 0, 0)),
            pl.BlockSpec((1, D_MODEL), lambda i: (0, 0)),
        ],
        out_specs=pl.BlockSpec((ROW_TILE, D_MODEL), lambda i: (i, 0)),
        out_shape=jax.ShapeDtypeStruct((s, D_MODEL), BF16),
        scratch_shapes=[pltpu.VMEM((ROW_TILE + POOL_HALO, D_MODEL), F32)],
        compiler_params=_params(("arbitrary",)),
        name="pool",
    )(h, norm_g, w_in, b_in, pool_w, pool_scale)


def _merge_kernel(*refs):
    h_refs = refs[:X_CHUNKS]
    (g_ref, a_ref, b_ref, wga_ref, wgb_ref, bga_ref, bgb_ref,
     wa_ref, wb_ref, wo_ref, o_ref, n_ref) = refs[X_CHUNKS:]
    j = pl.program_id(1)

    @pl.when(j == 0)
    def _():
        _norm_tile(h_refs, g_ref, n_ref, ROW_TILE, copy_ref=o_ref)

    n = n_ref[...]
    gate_a = jax.nn.sigmoid(_dot(n, wga_ref[...]) + bga_ref[...])
    gate_b = jax.nn.sigmoid(_dot(n, wgb_ref[...]) + bgb_ref[...])
    y_a = _dot(a_ref[...], wa_ref[...])
    y_b = _dot(b_ref[...], wb_ref[...])
    merged = (gate_a * y_a + gate_b * y_b).astype(BF16)
    o_ref[...] += _dot(merged, wo_ref[...])


def _merge(h, norm_g, a, b, w_in, b_in, w_a, w_b, w_o):
    s = h.shape[0]
    n_c = D_MODEL // COL_TILE
    ga0 = 3 * D_MODEL // COL_TILE
    gb0 = 4 * D_MODEL // COL_TILE
    row = lambda i, j: (i, 0)
    const = lambda i, j: (0, 0)
    colw = pl.BlockSpec((D_MODEL, COL_TILE), lambda i, j: (0, j))
    n_tiles = s // ROW_TILE
    return pl.pallas_call(
        _merge_kernel,
        grid=(n_tiles, n_c),
        in_specs=_staggered_row_specs(ROW_TILE, n_tiles, n_c) + [
            pl.BlockSpec((1, D_MODEL), const),
            pl.BlockSpec((ROW_TILE, D_MODEL), row),
            pl.BlockSpec((ROW_TILE, D_MODEL), row),
            pl.BlockSpec((D_MODEL, COL_TILE), lambda i, j: (0, ga0 + j)),
            pl.BlockSpec((D_MODEL, COL_TILE), lambda i, j: (0, gb0 + j)),
            pl.BlockSpec((1, COL_TILE), lambda i, j: (0, ga0 + j)),
            pl.BlockSpec((1, COL_TILE), lambda i, j: (0, gb0 + j)),
            colw,
            colw,
            pl.BlockSpec((COL_TILE, D_MODEL), lambda i, j: (j, 0)),
        ],
        out_specs=pl.BlockSpec((ROW_TILE, D_MODEL), row),
        out_shape=jax.ShapeDtypeStruct((s, D_MODEL), F32),
        scratch_shapes=[pltpu.VMEM((ROW_TILE, D_MODEL), BF16)],
        compiler_params=_params(("arbitrary", "arbitrary")),
        name="merge",
    )(*([h] * X_CHUNKS), norm_g, a, b, w_in, w_in, b_in, b_in, w_a, w_b, w_o)


def kernel(x, ffn1_norm, ffn1_w_in, ffn1_w_out, mix_norm, w_in, b_in, sgu_ln_g, sgu_ln_b, sgu_w_s, sgu_b_s, pool_w, pool_scale, w_branch_a, w_branch_b, w_out, ffn2_norm, ffn2_w_in, ffn2_w_out, final_norm):
    bsz, seq, d = x.shape
    assert d == D_MODEL and seq % ROW_TILE == 0 and (bsz * seq) % FFN_ROW_TILE == 0
    row2 = lambda v: v.reshape(1, -1)
    bf = lambda w: w.astype(BF16)

    h = x.reshape(bsz * seq, d)
    later_weights = (ffn2_w_in, ffn2_w_out, w_in, w_branch_a, w_branch_b, w_out)
    h, ffn2_w_in_bf, ffn2_w_out_bf, w_in_bf, w_a_bf, w_b_bf, w_o_bf = _ffn(
        h, row2(ffn1_norm), bf(ffn1_w_in), bf(ffn1_w_out), row2(final_norm),
        final_norm=False, cast_on_side=later_weights)
    b_in2 = row2(b_in)
    a = _sgu(h, row2(mix_norm), w_in_bf, b_in2, row2(sgu_ln_g), row2(sgu_ln_b), sgu_w_s, sgu_b_s[:, :, None])
    b = _pool(h, row2(mix_norm), w_in_bf, b_in2, bf(pool_w), row2(pool_scale), seq=seq)
    h = _merge(h, row2(mix_norm), a, b, w_in_bf, b_in2, w_a_bf, w_b_bf, w_o_bf)
    h, = _ffn(h, row2(ffn2_norm), ffn2_w_in_bf, ffn2_w_out_bf, row2(final_norm), final_norm=True)
    return h.reshape(bsz, seq, d).astype(x.dtype)
```

```python
import functools

import jax
import jax.numpy as jnp
from jax import lax
from jax.experimental import pallas as pl
from jax.experimental.pallas import tpu as pltpu

D_MODEL = 2048
D_FF = 5632
CHUNK = 64
SGU_BLOCK = 128
SGU_HEADS = 8
SGU_HEAD_DIM = D_MODEL // SGU_HEADS
POOL_WINDOWS = (2, 4, 8, 16)
POOL_GROUP_DIM = D_MODEL // len(POOL_WINDOWS)
POOL_HALO = 16
RMS_EPS = 1e-6
LN_EPS = 1e-5

ROW_TILE = 512
FFN_ROW_TILE = 1024
NORM_ROWS = 128
COL_TILE = 512
SGU_ROW_TILE = 512
SGU_COL_TILE = 2048
SGU_STEPS = D_MODEL // SGU_COL_TILE
V7X_VMEM_LIMIT_BYTES = 60 * 1024 * 1024

F32 = jnp.float32
BF16 = jnp.bfloat16
LANES = 128
BF16_SUBLANES = 16


def _rms_bf16(x, g):
    y = x * lax.rsqrt(jnp.mean(x * x, axis=-1, keepdims=True) + RMS_EPS)
    return (y * g).astype(BF16)


def _side_cast_specs(shape, n_tiles, n_steps):
    rows, cols = shape
    assert rows % n_tiles == 0 and (rows // n_tiles) % BF16_SUBLANES == 0
    n_col = max(c for c in range(1, n_steps + 1) if cols % c == 0 and (cols // c) % LANES == 0)
    index = lambda i, j: (i, jnp.minimum(j, n_col - 1))
    return pl.BlockSpec((rows // n_tiles, cols // n_col), index)


def _norm_tile(x_ref, g_ref, n_ref, rows, copy_ref=None):
    def step(r, carry):
        rs = pl.ds(pl.multiple_of(r * NORM_ROWS, NORM_ROWS), NORM_ROWS)
        x = x_ref[rs, :]
        rstd = lax.rsqrt(jnp.mean(x * x, axis=-1, keepdims=True) + RMS_EPS)
        n_ref[rs, :] = (x * rstd * g_ref[...]).astype(BF16)
        if copy_ref is not None:
            copy_ref[rs, :] = x
        return carry
    lax.fori_loop(0, rows // NORM_ROWS, step, None)


def _gelu(x):
    return 0.5 * x * (1.0 + lax.erf(x * (2.0 ** -0.5)))


def _dot(a, b):
    return jnp.dot(a, b, preferred_element_type=F32)


def _params(semantics):
    return pltpu.CompilerParams(dimension_semantics=semantics,
                                vmem_limit_bytes=V7X_VMEM_LIMIT_BYTES)


def _ffn_kernel(*refs, final_norm, n_side):
    x_ref, g_ref, wg_ref, wu_ref, wo_ref, fin_ref = refs[:6]
    side_in = refs[6:6 + n_side]
    o_ref = refs[6 + n_side]
    side_out = refs[7 + n_side:7 + 2 * n_side]
    xn_ref = refs[-1]
    j = pl.program_id(1)

    @pl.when(j == 0)
    def _():
        _norm_tile(x_ref, g_ref, xn_ref, FFN_ROW_TILE, copy_ref=o_ref)

    for src, dst in zip(side_in, side_out):
        dst[...] = src[...].astype(BF16)

    xn = xn_ref[...]
    gate = _dot(xn, wg_ref[...])
    up = _dot(xn, wu_ref[...])
    act = (0.5 * gate * jax.nn.sigmoid(gate) * up).astype(BF16)
    o_ref[...] += _dot(act, wo_ref[...])

    if final_norm:
        @pl.when(j == pl.num_programs(1) - 1)
        def _():
            def step(r, carry):
                rs = pl.ds(pl.multiple_of(r * NORM_ROWS, NORM_ROWS), NORM_ROWS)
                h = o_ref[rs, :]
                o_ref[rs, :] = h * lax.rsqrt(jnp.mean(h * h, axis=-1, keepdims=True) + RMS_EPS) * fin_ref[...]
                return carry
            lax.fori_loop(0, FFN_ROW_TILE // NORM_ROWS, step, None)


def _ffn(x, norm_g, w_in, w_out, fin_g, *, final_norm, cast_on_side=()):
    s = x.shape[0]
    n_ff = D_FF // COL_TILE
    n_tiles = s // FFN_ROW_TILE
    row = lambda i, j: (i, 0)
    const = lambda i, j: (0, 0)
    side_specs = [_side_cast_specs(w.shape, n_tiles, n_ff) for w in cast_on_side]
    return pl.pallas_call(
        functools.partial(_ffn_kernel, final_norm=final_norm, n_side=len(cast_on_side)),
        grid=(n_tiles, n_ff),
        in_specs=[
            pl.BlockSpec((FFN_ROW_TILE, D_MODEL), row),
            pl.BlockSpec((1, D_MODEL), const),
            pl.BlockSpec((D_MODEL, COL_TILE), lambda i, j: (0, j)),
            pl.BlockSpec((D_MODEL, COL_TILE), lambda i, j: (0, j + n_ff)),
            pl.BlockSpec((COL_TILE, D_MODEL), lambda i, j: (j, 0)),
            pl.BlockSpec((1, D_MODEL), const),
        ] + side_specs,
        out_specs=[pl.BlockSpec((FFN_ROW_TILE, D_MODEL), row)] + side_specs,
        out_shape=[jax.ShapeDtypeStruct((s, D_MODEL), F32)]
                  + [jax.ShapeDtypeStruct(w.shape, BF16) for w in cast_on_side],
        scratch_shapes=[pltpu.VMEM((FFN_ROW_TILE, D_MODEL), BF16)],
        compiler_params=_params(("arbitrary", "arbitrary")),
        name="ffn_final" if final_norm else "ffn",
    )(x, norm_g, w_in, w_in, w_out, fin_g, *cast_on_side)


def _sgu_kernel(h_ref, g_ref, w_ref, b_ref, lng_ref, lnb_ref, ws_ref, bs_ref,
                a_ref, n_ref, v_ref, mu_ref, rstd_ref):
    j = pl.program_id(1)

    @pl.when(j == 0)
    def _():
        _norm_tile(h_ref, g_ref, n_ref, SGU_ROW_TILE)

    @pl.when(j < SGU_STEPS)
    def _():
        col = pl.multiple_of(j * SGU_COL_TILE, SGU_COL_TILE)
        v_ref[:, pl.ds(col, SGU_COL_TILE)] = _gelu(_dot(n_ref[...], w_ref[...]) + b_ref[...])

    @pl.when(j == SGU_STEPS)
    def _():
        v = v_ref[...]
        mu = jnp.mean(v, axis=-1, keepdims=True)
        mu_ref[...] = mu
        rstd_ref[...] = lax.rsqrt(jnp.mean(jnp.square(v - mu), axis=-1, keepdims=True) + LN_EPS)

    @pl.when(j >= SGU_STEPS)
    def _():
        c = j - SGU_STEPS
        col = pl.multiple_of(c * SGU_COL_TILE, SGU_COL_TILE)
        u = _gelu(_dot(n_ref[...], w_ref[...]) + b_ref[...])
        v = v_ref[:, pl.ds(col, SGU_COL_TILE)]
        vln = ((v - mu_ref[...]) * rstd_ref[...] * lng_ref[:, pl.ds(col, SGU_COL_TILE)]
               + lnb_ref[:, pl.ds(col, SGU_COL_TILE)]).astype(BF16)
        pi = lax.broadcasted_iota(jnp.int32, (SGU_BLOCK, SGU_BLOCK), 0) // CHUNK
        pj = lax.broadcasted_iota(jnp.int32, (SGU_BLOCK, SGU_BLOCK), 1) // CHUNK
        mask = (pj <= pi).astype(F32)
        heads_per_step = SGU_COL_TILE // SGU_HEAD_DIM
        for hh in range(heads_per_step):
            head = c * heads_per_step + hh
            w = (ws_ref[head] * mask).astype(BF16)
            bias = bs_ref[head]
            cs = slice(hh * SGU_HEAD_DIM, (hh + 1) * SGU_HEAD_DIM)
            for blk in range(SGU_ROW_TILE // SGU_BLOCK):
                rs = slice(blk * SGU_BLOCK, (blk + 1) * SGU_BLOCK)
                sp = _dot(w, vln[rs, cs]) + bias
                a_ref[rs, cs] = (u[rs, cs] * sp).astype(BF16)


def _sgu(h, norm_g, w_in, b_in, ln_g, ln_b, w_s, b_s):
    s = h.shape[0]
    const = lambda i, j: (0, 0)
    const3 = lambda i, j: (0, 0, 0)
    wcol = lambda i, j: (0, (j + SGU_STEPS) % (2 * SGU_STEPS))
    n_tiles = s // SGU_ROW_TILE
    return pl.pallas_call(
        _sgu_kernel,
        grid=(n_tiles, 2 * SGU_STEPS),
        in_specs=[
            pl.BlockSpec((SGU_ROW_TILE, D_MODEL), lambda i, j: (i, 0)),
            pl.BlockSpec((1, D_MODEL), const),
            pl.BlockSpec((D_MODEL, SGU_COL_TILE), wcol),
            pl.BlockSpec((1, SGU_COL_TILE), wcol),
            pl.BlockSpec((1, D_MODEL), const),
            pl.BlockSpec((1, D_MODEL), const),
            pl.BlockSpec((SGU_HEADS, SGU_BLOCK, SGU_BLOCK), const3),
            pl.BlockSpec((SGU_HEADS, SGU_BLOCK, 1), const3),
        ],
        out_specs=pl.BlockSpec((SGU_ROW_TILE, SGU_COL_TILE),
                               lambda i, j: (i, jnp.maximum(j - SGU_STEPS, 0))),
        out_shape=jax.ShapeDtypeStruct((s, D_MODEL), BF16),
        scratch_shapes=[pltpu.VMEM((SGU_ROW_TILE, D_MODEL), BF16),
                        pltpu.VMEM((SGU_ROW_TILE, D_MODEL), F32),
                        pltpu.VMEM((SGU_ROW_TILE, 1), F32),
                        pltpu.VMEM((SGU_ROW_TILE, 1), F32)],
        compiler_params=_params(("arbitrary", "arbitrary")),
        name="sgu",
    )(h, norm_g, w_in, b_in, ln_g, ln_b, w_s, b_s)


def _pool_kernel(h_ref, g_ref, w_ref, b_ref, pw_ref, ps_ref, o_ref, z_ref, *, seq):
    t0 = (pl.program_id(0) * ROW_TILE) % seq

    @pl.when(t0 == 0)
    def _():
        z_ref[0:POOL_HALO, :] = jnp.zeros((POOL_HALO, D_MODEL), F32)

    n = _rms_bf16(h_ref[...], g_ref[...])
    t = t0 + lax.broadcasted_iota(jnp.int32, (ROW_TILE, 1), 0)
    for k, win in reversed(list(enumerate(POOL_WINDOWS))):
        cs = slice(k * POOL_GROUP_DIM, (k + 1) * POOL_GROUP_DIM)
        z = _dot(n, w_ref[:, cs]) + b_ref[:, cs]
        z_ref[POOL_HALO:, cs] = z
        acc = z_ref[:, cs]
        d = 1
        while d < win:
            acc = acc + pltpu.roll(acc, d, axis=0)
            d *= 2
        inv_cnt = 1.0 / jnp.minimum(t + 1, win).astype(F32)
        pooled = (acc[POOL_HALO:] * inv_cnt - z).astype(BF16)
        mixed = _dot(pooled, pw_ref[k])
        o_ref[:, cs] = (mixed * ps_ref[:, cs]).astype(BF16)
    z_ref[0:POOL_HALO, :] = z_ref[ROW_TILE:ROW_TILE + POOL_HALO, :]


def _pool(h, norm_g, w_in, b_in, pool_w, pool_scale, *, seq):
    s = h.shape[0]
    zb_block = 2 * D_MODEL // D_MODEL
    return pl.pallas_call(
        functools.partial(_pool_kernel, seq=seq),
        grid=(s // ROW_TILE,),
        in_specs=[
            pl.BlockSpec((ROW_TILE, D_MODEL), lambda i: (i, 0)),
            pl.BlockSpec((1, D_MODEL), lambda i: (0, 0)),
            pl.BlockSpec((D_MODEL, D_MODEL), lambda i: (0, zb_block)),
            pl.BlockSpec((1, D_MODEL), lambda i: (0, zb_block)),
            pl.BlockSpec((len(POOL_WINDOWS), POOL_GROUP_DIM, POOL_GROUP_DIM), lambda i: (0, 0, 0)),
            pl.BlockSpec((1, D_MODEL), lambda i: (0, 0)),
        ],
        out_specs=pl.BlockSpec((ROW_TILE, D_MODEL), lambda i: (i, 0)),
        out_shape=jax.ShapeDtypeStruct((s, D_MODEL), BF16),
        scratch_shapes=[pltpu.VMEM((ROW_TILE + POOL_HALO, D_MODEL), F32)],
        compiler_params=_params(("arbitrary",)),
        name="pool",
    )(h, norm_g, w_in, b_in, pool_w, pool_scale)


def _merge_kernel(h_ref, g_ref, a_ref, b_ref, wga_ref, wgb_ref, bga_ref, bgb_ref,
                  wa_ref, wb_ref, wo_ref, o_ref, n_ref):
    j = pl.program_id(1)

    @pl.when(j == 0)
    def _():
        _norm_tile(h_ref, g_ref, n_ref, ROW_TILE, copy_ref=o_ref)

    n = n_ref[...]
    gate_a = jax.nn.sigmoid(_dot(n, wga_ref[...]) + bga_ref[...])
    gate_b = jax.nn.sigmoid(_dot(n, wgb_ref[...]) + bgb_ref[...])
    y_a = _dot(a_ref[...], wa_ref[...])
    y_b = _dot(b_ref[...], wb_ref[...])
    merged = (gate_a * y_a + gate_b * y_b).astype(BF16)
    o_ref[...] += _dot(merged, wo_ref[...])


def _merge(h, norm_g, a, b, w_in, b_in, w_a, w_b, w_o):
    s = h.shape[0]
    n_c = D_MODEL // COL_TILE
    ga0 = 3 * D_MODEL // COL_TILE
    gb0 = 4 * D_MODEL // COL_TILE
    row = lambda i, j: (i, 0)
    const = lambda i, j: (0, 0)
    colw = pl.BlockSpec((D_MODEL, COL_TILE), lambda i, j: (0, j))
    n_tiles = s // ROW_TILE
    return pl.pallas_call(
        _merge_kernel,
        grid=(n_tiles, n_c),
        in_specs=[
            pl.BlockSpec((ROW_TILE, D_MODEL), row),
            pl.BlockSpec((1, D_MODEL), const),
            pl.BlockSpec((ROW_TILE, D_MODEL), row),
            pl.BlockSpec((ROW_TILE, D_MODEL), row),
            pl.BlockSpec((D_MODEL, COL_TILE), lambda i, j: (0, ga0 + j)),
            pl.BlockSpec((D_MODEL, COL_TILE), lambda i, j: (0, gb0 + j)),
            pl.BlockSpec((1, COL_TILE), lambda i, j: (0, ga0 + j)),
            pl.BlockSpec((1, COL_TILE), lambda i, j: (0, gb0 + j)),
            colw,
            colw,
            pl.BlockSpec((COL_TILE, D_MODEL), lambda i, j: (j, 0)),
        ],
        out_specs=pl.BlockSpec((ROW_TILE, D_MODEL), row),
        out_shape=jax.ShapeDtypeStruct((s, D_MODEL), F32),
        scratch_shapes=[pltpu.VMEM((ROW_TILE, D_MODEL), BF16)],
        compiler_params=_params(("arbitrary", "arbitrary")),
        name="merge",
    )(h, norm_g, a, b, w_in, w_in, b_in, b_in, w_a, w_b, w_o)


def kernel(x, ffn1_norm, ffn1_w_in, ffn1_w_out, mix_norm, w_in, b_in, sgu_ln_g, sgu_ln_b, sgu_w_s, sgu_b_s, pool_w, pool_scale, w_branch_a, w_branch_b, w_out, ffn2_norm, ffn2_w_in, ffn2_w_out, final_norm):
    bsz, seq, d = x.shape
    assert d == D_MODEL and seq % ROW_TILE == 0 and (bsz * seq) % FFN_ROW_TILE == 0
    row2 = lambda v: v.reshape(1, -1)
    bf = lambda w: w.astype(BF16)

    h = x.reshape(bsz * seq, d)
    later_weights = (ffn2_w_in, ffn2_w_out, w_in, w_branch_a, w_branch_b, w_out)
    h, ffn2_w_in_bf, ffn2_w_out_bf, w_in_bf, w_a_bf, w_b_bf, w_o_bf = _ffn(
        h, row2(ffn1_norm), bf(ffn1_w_in), bf(ffn1_w_out), row2(final_norm),
        final_norm=False, cast_on_side=later_weights)
    b_in2 = row2(b_in)
    a = _sgu(h, row2(mix_norm), w_in_bf, b_in2, row2(sgu_ln_g), row2(sgu_ln_b), sgu_w_s, sgu_b_s[:, :, None])
    b = _pool(h, row2(mix_norm), w_in_bf, b_in2, bf(pool_w), row2(pool_scale), seq=seq)
    h = _merge(h, row2(mix_norm), a, b, w_in_bf, b_in2, w_a_bf, w_b_bf, w_o_bf)
    h, = _ffn(h, row2(ffn2_norm), ffn2_w_in_bf, ffn2_w_out_bf, row2(final_norm), final_norm=True)
    return h.reshape(bsz, seq, d).astype(x.dtype)
```

```python
import functools

import jax
import jax.numpy as jnp
from jax import lax
from jax.experimental import pallas as pl
from jax.experimental.pallas import tpu as pltpu

D_MODEL = 2048
D_FF = 5632
CHUNK = 64
SGU_BLOCK = 128
SGU_HEADS = 8
SGU_HEAD_DIM = D_MODEL // SGU_HEADS
POOL_WINDOWS = (2, 4, 8, 16)
POOL_GROUP_DIM = D_MODEL // len(POOL_WINDOWS)
POOL_HALO = 16
RMS_EPS = 1e-6
LN_EPS = 1e-5

ROW_TILE = 512
FFN_ROW_TILE = 1024
NORM_ROWS = 128
COL_TILE = 512
V7X_VMEM_LIMIT_BYTES = 60 * 1024 * 1024

F32 = jnp.float32
BF16 = jnp.bfloat16
LANES = 128
BF16_SUBLANES = 16


def _side_cast_specs(shape, n_tiles, n_steps):
    rows, cols = shape
    assert rows % n_tiles == 0 and (rows // n_tiles) % BF16_SUBLANES == 0
    n_col = max(c for c in range(1, n_steps + 1) if cols % c == 0 and (cols // c) % LANES == 0)
    index = lambda i, j: (i, jnp.minimum(j, n_col - 1))
    return pl.BlockSpec((rows // n_tiles, cols // n_col), index)


def _norm_tile(x_ref, g_ref, n_ref, rows, copy_ref=None):
    def step(r, carry):
        rs = pl.ds(pl.multiple_of(r * NORM_ROWS, NORM_ROWS), NORM_ROWS)
        x = x_ref[rs, :]
        rstd = lax.rsqrt(jnp.mean(x * x, axis=-1, keepdims=True) + RMS_EPS)
        n_ref[rs, :] = (x * rstd * g_ref[...]).astype(BF16)
        if copy_ref is not None:
            copy_ref[rs, :] = x
        return carry
    lax.fori_loop(0, rows // NORM_ROWS, step, None)


def _gelu(x):
    return 0.5 * x * (1.0 + lax.erf(x * (2.0 ** -0.5)))


def _dot(a, b):
    return jnp.dot(a, b, preferred_element_type=F32)


def _params(semantics):
    return pltpu.CompilerParams(dimension_semantics=semantics,
                                vmem_limit_bytes=V7X_VMEM_LIMIT_BYTES)


def _ffn_kernel(*refs, final_norm, n_side):
    x_ref, g_ref, wg_ref, wu_ref, wo_ref, fin_ref = refs[:6]
    side_in = refs[6:6 + n_side]
    o_ref = refs[6 + n_side]
    side_out = refs[7 + n_side:7 + 2 * n_side]
    xn_ref = refs[-1]
    j = pl.program_id(1)

    @pl.when(j == 0)
    def _():
        _norm_tile(x_ref, g_ref, xn_ref, FFN_ROW_TILE, copy_ref=o_ref)

    for src, dst in zip(side_in, side_out):
        dst[...] = src[...].astype(BF16)

    xn = xn_ref[...]
    gate = _dot(xn, wg_ref[...])
    up = _dot(xn, wu_ref[...])
    act = (0.5 * gate * jax.nn.sigmoid(gate) * up).astype(BF16)
    o_ref[...] += _dot(act, wo_ref[...])

    if final_norm:
        @pl.when(j == pl.num_programs(1) - 1)
        def _():
            def step(r, carry):
                rs = pl.ds(pl.multiple_of(r * NORM_ROWS, NORM_ROWS), NORM_ROWS)
                h = o_ref[rs, :]
                o_ref[rs, :] = h * lax.rsqrt(jnp.mean(h * h, axis=-1, keepdims=True) + RMS_EPS) * fin_ref[...]
                return carry
            lax.fori_loop(0, FFN_ROW_TILE // NORM_ROWS, step, None)


def _ffn(x, norm_g, w_in, w_out, fin_g, *, final_norm, cast_on_side=()):
    s = x.shape[0]
    n_ff = D_FF // COL_TILE
    n_tiles = s // FFN_ROW_TILE
    row = lambda i, j: (i, 0)
    const = lambda i, j: (0, 0)
    side_specs = [_side_cast_specs(w.shape, n_tiles, n_ff) for w in cast_on_side]
    return pl.pallas_call(
        functools.partial(_ffn_kernel, final_norm=final_norm, n_side=len(cast_on_side)),
        grid=(n_tiles, n_ff),
        in_specs=[
            pl.BlockSpec((FFN_ROW_TILE, D_MODEL), row),
            pl.BlockSpec((1, D_MODEL), const),
            pl.BlockSpec((D_MODEL, COL_TILE), lambda i, j: (0, j)),
            pl.BlockSpec((D_MODEL, COL_TILE), lambda i, j: (0, j + n_ff)),
            pl.BlockSpec((COL_TILE, D_MODEL), lambda i, j: (j, 0)),
            pl.BlockSpec((1, D_MODEL), const),
        ] + side_specs,
        out_specs=[pl.BlockSpec((FFN_ROW_TILE, D_MODEL), row)] + side_specs,
        out_shape=[jax.ShapeDtypeStruct((s, D_MODEL), F32)]
                  + [jax.ShapeDtypeStruct(w.shape, BF16) for w in cast_on_side],
        scratch_shapes=[pltpu.VMEM((FFN_ROW_TILE, D_MODEL), BF16)],
        compiler_params=_params(("arbitrary", "arbitrary")),
        name="ffn_final" if final_norm else "ffn",
    )(x, norm_g, w_in, w_in, w_out, fin_g, *cast_on_side)


def _mix_kernel(h_ref, g_ref, w_ref, b_ref, lng_ref, lnb_ref, ws_ref, bs_ref, pw_ref, ps_ref,
                n_ref, a_ref, p_ref, v_ref, z_ref, *, seq):
    i, j = pl.program_id(0), pl.program_id(1)

    @pl.when(j == 0)
    def _():
        _norm_tile(h_ref, g_ref, n_ref, ROW_TILE)
        v_ref[...] = _gelu(_dot(n_ref[...], w_ref[...]) + b_ref[...])

    @pl.when(j == 1)
    def _():
        n = n_ref[...]
        v = v_ref[...]
        vc = v - jnp.mean(v, axis=-1, keepdims=True)
        rstd = lax.rsqrt(jnp.mean(jnp.square(vc), axis=-1, keepdims=True) + LN_EPS)
        vln = (vc * rstd * lng_ref[...] + lnb_ref[...]).astype(BF16)
        pi = lax.broadcasted_iota(jnp.int32, (SGU_BLOCK, SGU_BLOCK), 0) // CHUNK
        pj = lax.broadcasted_iota(jnp.int32, (SGU_BLOCK, SGU_BLOCK), 1) // CHUNK
        mask = (pj <= pi).astype(F32)
        for head in range(SGU_HEADS):
            cs = slice(head * SGU_HEAD_DIM, (head + 1) * SGU_HEAD_DIM)
            u = _gelu(_dot(n, w_ref[:, cs]) + b_ref[:, cs])
            w = (ws_ref[head] * mask).astype(BF16)
            bias = bs_ref[head]
            for blk in range(ROW_TILE // SGU_BLOCK):
                rs = slice(blk * SGU_BLOCK, (blk + 1) * SGU_BLOCK)
                sp = _dot(w, vln[rs, cs]) + bias
                a_ref[rs, cs] = (u[rs] * sp).astype(BF16)

    t0 = (i * ROW_TILE) % seq

    @pl.when((j == 2) & (t0 == 0))
    def _():
        z_ref[0:POOL_HALO, :] = jnp.zeros((POOL_HALO, D_MODEL), F32)

    @pl.when(j == 2)
    def _():
        n = n_ref[...]
        t = t0 + lax.broadcasted_iota(jnp.int32, (ROW_TILE, 1), 0)
        for k, win in reversed(list(enumerate(POOL_WINDOWS))):
            cs = slice(k * POOL_GROUP_DIM, (k + 1) * POOL_GROUP_DIM)
            z = _dot(n, w_ref[:, cs]) + b_ref[:, cs]
            z_ref[POOL_HALO:, cs] = z
            acc = z_ref[:, cs]
            d = 1
            while d < win:
                acc = acc + pltpu.roll(acc, d, axis=0)
                d *= 2
            inv_cnt = 1.0 / jnp.minimum(t + 1, win).astype(F32)
            pooled = (acc[POOL_HALO:] * inv_cnt - z).astype(BF16)
            mixed = _dot(pooled, pw_ref[k])
            p_ref[:, cs] = (mixed * ps_ref[:, cs]).astype(BF16)
        z_ref[0:POOL_HALO, :] = z_ref[ROW_TILE:ROW_TILE + POOL_HALO, :]


def _mix(h, norm_g, w_in, b_in, ln_g, ln_b, w_s, b_s, pool_w, pool_scale, *, seq):
    s = h.shape[0]
    row = lambda i, j: (i, 0)
    const = lambda i, j: (0, 0)
    const3 = lambda i, j: (0, 0, 0)
    wcol = lambda i, j: (0, jnp.where(j < 2, 1 - j, j))
    act = pl.BlockSpec((ROW_TILE, D_MODEL), row)
    return pl.pallas_call(
        functools.partial(_mix_kernel, seq=seq),
        grid=(s // ROW_TILE, 3),
        in_specs=[
            act,
            pl.BlockSpec((1, D_MODEL), const),
            pl.BlockSpec((D_MODEL, D_MODEL), wcol),
            pl.BlockSpec((1, D_MODEL), wcol),
            pl.BlockSpec((1, D_MODEL), const),
            pl.BlockSpec((1, D_MODEL), const),
            pl.BlockSpec((SGU_HEADS, SGU_BLOCK, SGU_BLOCK), const3),
            pl.BlockSpec((SGU_HEADS, SGU_BLOCK, 1), const3),
            pl.BlockSpec((len(POOL_WINDOWS), POOL_GROUP_DIM, POOL_GROUP_DIM), const3),
            pl.BlockSpec((1, D_MODEL), const),
        ],
        out_specs=[act, act, act],
        out_shape=[jax.ShapeDtypeStruct((s, D_MODEL), BF16)] * 3,
        scratch_shapes=[pltpu.VMEM((ROW_TILE, D_MODEL), F32),
                        pltpu.VMEM((ROW_TILE + POOL_HALO, D_MODEL), F32)],
        compiler_params=_params(("arbitrary", "arbitrary")),
        name="mix",
    )(h, norm_g, w_in, b_in, ln_g, ln_b, w_s, b_s, pool_w, pool_scale)


def _merge_kernel(h_ref, n_ref, a_ref, b_ref, wga_ref, wgb_ref, bga_ref, bgb_ref,
                  wa_ref, wb_ref, wo_ref, o_ref):
    @pl.when(pl.program_id(1) == 0)
    def _():
        o_ref[...] = h_ref[...]

    n = n_ref[...]
    gate_a = jax.nn.sigmoid(_dot(n, wga_ref[...]) + bga_ref[...])
    gate_b = jax.nn.sigmoid(_dot(n, wgb_ref[...]) + bgb_ref[...])
    y_a = _dot(a_ref[...], wa_ref[...])
    y_b = _dot(b_ref[...], wb_ref[...])
    merged = (gate_a * y_a + gate_b * y_b).astype(BF16)
    o_ref[...] += _dot(merged, wo_ref[...])


def _merge(h, n, a, b, w_in, b_in, w_a, w_b, w_o):
    s = h.shape[0]
    n_c = D_MODEL // COL_TILE
    ga0 = 3 * D_MODEL // COL_TILE
    gb0 = 4 * D_MODEL // COL_TILE
    row = lambda i, j: (i, 0)
    act = pl.BlockSpec((ROW_TILE, D_MODEL), row)
    colw = pl.BlockSpec((D_MODEL, COL_TILE), lambda i, j: (0, j))
    return pl.pallas_call(
        _merge_kernel,
        grid=(s // ROW_TILE, n_c),
        in_specs=[
            act, act, act, act,
            pl.BlockSpec((D_MODEL, COL_TILE), lambda i, j: (0, ga0 + j)),
            pl.BlockSpec((D_MODEL, COL_TILE), lambda i, j: (0, gb0 + j)),
            pl.BlockSpec((1, COL_TILE), lambda i, j: (0, ga0 + j)),
            pl.BlockSpec((1, COL_TILE), lambda i, j: (0, gb0 + j)),
            colw,
            colw,
            pl.BlockSpec((COL_TILE, D_MODEL), lambda i, j: (j, 0)),
        ],
        out_specs=act,
        out_shape=jax.ShapeDtypeStruct((s, D_MODEL), F32),
        compiler_params=_params(("arbitrary", "arbitrary")),
        name="merge",
    )(h, n, a, b, w_in, w_in, b_in, b_in, w_a, w_b, w_o)


def kernel(x, ffn1_norm, ffn1_w_in, ffn1_w_out, mix_norm, w_in, b_in, sgu_ln_g, sgu_ln_b, sgu_w_s, sgu_b_s, pool_w, pool_scale, w_branch_a, w_branch_b, w_out, ffn2_norm, ffn2_w_in, ffn2_w_out, final_norm):
    bsz, seq, d = x.shape
    assert d == D_MODEL and seq % ROW_TILE == 0 and (bsz * seq) % FFN_ROW_TILE == 0
    row2 = lambda v: v.reshape(1, -1)
    bf = lambda w: w.astype(BF16)

    h = x.reshape(bsz * seq, d)
    later_weights = (ffn2_w_in, ffn2_w_out, w_in, w_branch_a, w_branch_b, w_out)
    h, ffn2_w_in_bf, ffn2_w_out_bf, w_in_bf, w_a_bf, w_b_bf, w_o_bf = _ffn(
        h, row2(ffn1_norm), bf(ffn1_w_in), bf(ffn1_w_out), row2(final_norm),
        final_norm=False, cast_on_side=later_weights)
    b_in2 = row2(b_in)
    n, a, b = _mix(h, row2(mix_norm), w_in_bf, b_in2, row2(sgu_ln_g), row2(sgu_ln_b), sgu_w_s,
                   sgu_b_s[:, :, None], bf(pool_w), row2(pool_scale), seq=seq)
    h = _merge(h, n, a, b, w_in_bf, b_in2, w_a_bf, w_b_bf, w_o_bf)
    h, = _ffn(h, row2(ffn2_norm), ffn2_w_in_bf, ffn2_w_out_bf, row2(final_norm), final_norm=True)
    return h.reshape(bsz, seq, d).astype(x.dtype)
```

```python
import functools

import jax
import jax.numpy as jnp
from jax import lax
from jax.experimental import pallas as pl
from jax.experimental.pallas import tpu as pltpu

D_MODEL = 2048
D_FF = 5632
CHUNK = 64
SGU_BLOCK = 128
SGU_HEADS = 8
SGU_HEAD_DIM = D_MODEL // SGU_HEADS
POOL_WINDOWS = (2, 4, 8, 16)
POOL_GROUP_DIM = D_MODEL // len(POOL_WINDOWS)
POOL_HALO = 16
RMS_EPS = 1e-6
LN_EPS = 1e-5

ROW_TILE = 512
FFN_ROW_TILE = 1024
NORM_ROWS = 128
COL_TILE = 512
V7X_VMEM_LIMIT_BYTES = 60 * 1024 * 1024

F32 = jnp.float32
BF16 = jnp.bfloat16
LANES = 128
BF16_SUBLANES = 16


def _side_cast_specs(shape, n_tiles, n_steps):
    rows, cols = shape
    assert rows % n_tiles == 0 and (rows // n_tiles) % BF16_SUBLANES == 0
    n_col = max(c for c in range(1, n_steps + 1) if cols % c == 0 and (cols // c) % LANES == 0)
    index = lambda i, j: (i, jnp.minimum(j, n_col - 1))
    return pl.BlockSpec((rows // n_tiles, cols // n_col), index)


def _norm_tile(x_ref, g_ref, n_ref, rows, copy_ref=None):
    def step(r, carry):
        rs = pl.ds(pl.multiple_of(r * NORM_ROWS, NORM_ROWS), NORM_ROWS)
        x = x_ref[rs, :]
        rstd = lax.rsqrt(jnp.mean(x * x, axis=-1, keepdims=True) + RMS_EPS)
        n_ref[rs, :] = (x * rstd * g_ref[...]).astype(BF16)
        if copy_ref is not None:
            copy_ref[rs, :] = x
        return carry
    lax.fori_loop(0, rows // NORM_ROWS, step, None)


def _gelu(x):
    return 0.5 * x * (1.0 + lax.erf(x * (2.0 ** -0.5)))


def _dot(a, b):
    return jnp.dot(a, b, preferred_element_type=F32)


def _params(semantics):
    return pltpu.CompilerParams(dimension_semantics=semantics,
                                vmem_limit_bytes=V7X_VMEM_LIMIT_BYTES)


def _ffn_kernel(*refs, final_norm, n_side):
    x_ref, g_ref, wg_ref, wu_ref, wo_ref, fin_ref = refs[:6]
    side_in = refs[6:6 + n_side]
    o_ref = refs[6 + n_side]
    side_out = refs[7 + n_side:7 + 2 * n_side]
    xn_ref = refs[-1]
    j = pl.program_id(1)

    @pl.when(j == 0)
    def _():
        _norm_tile(x_ref, g_ref, xn_ref, FFN_ROW_TILE, copy_ref=o_ref)

    for src, dst in zip(side_in, side_out):
        dst[...] = src[...].astype(BF16)

    xn = xn_ref[...]
    gate = _dot(xn, wg_ref[...])
    up = _dot(xn, wu_ref[...])
    act = (0.5 * gate * jax.nn.sigmoid(gate) * up).astype(BF16)
    o_ref[...] += _dot(act, wo_ref[...])

    if final_norm:
        @pl.when(j == pl.num_programs(1) - 1)
        def _():
            def step(r, carry):
                rs = pl.ds(pl.multiple_of(r * NORM_ROWS, NORM_ROWS), NORM_ROWS)
                h = o_ref[rs, :]
                o_ref[rs, :] = h * lax.rsqrt(jnp.mean(h * h, axis=-1, keepdims=True) + RMS_EPS) * fin_ref[...]
                return carry
            lax.fori_loop(0, FFN_ROW_TILE // NORM_ROWS, step, None)


def _ffn(x, norm_g, w_in, w_out, fin_g, *, final_norm, cast_on_side=()):
    s = x.shape[0]
    n_ff = D_FF // COL_TILE
    n_tiles = s // FFN_ROW_TILE
    row = lambda i, j: (i, 0)
    const = lambda i, j: (0, 0)
    side_specs = [_side_cast_specs(w.shape, n_tiles, n_ff) for w in cast_on_side]
    return pl.pallas_call(
        functools.partial(_ffn_kernel, final_norm=final_norm, n_side=len(cast_on_side)),
        grid=(n_tiles, n_ff),
        in_specs=[
            pl.BlockSpec((FFN_ROW_TILE, D_MODEL), row),
            pl.BlockSpec((1, D_MODEL), const),
            pl.BlockSpec((D_MODEL, COL_TILE), lambda i, j: (0, j)),
            pl.BlockSpec((D_MODEL, COL_TILE), lambda i, j: (0, j + n_ff)),
            pl.BlockSpec((COL_TILE, D_MODEL), lambda i, j: (j, 0)),
            pl.BlockSpec((1, D_MODEL), const),
        ] + side_specs,
        out_specs=[pl.BlockSpec((FFN_ROW_TILE, D_MODEL), row)] + side_specs,
        out_shape=[jax.ShapeDtypeStruct((s, D_MODEL), F32)]
                  + [jax.ShapeDtypeStruct(w.shape, BF16) for w in cast_on_side],
        scratch_shapes=[pltpu.VMEM((FFN_ROW_TILE, D_MODEL), BF16)],
        compiler_params=_params(("arbitrary", "arbitrary")),
        name="ffn_final" if final_norm else "ffn",
    )(x, norm_g, w_in, w_in, w_out, fin_g, *cast_on_side)


def _mix_kernel(h_ref, g_ref, w_ref, b_ref, lng_ref, lnb_ref, ws_ref, bs_ref, pw_ref, ps_ref,
                n_ref, a_ref, p_ref, v_ref, z_ref, *, seq):
    i, j = pl.program_id(0), pl.program_id(1)

    @pl.when(j == 0)
    def _():
        _norm_tile(h_ref, g_ref, n_ref, ROW_TILE)
        v_ref[...] = _gelu(_dot(n_ref[...], w_ref[...]) + b_ref[...])

    @pl.when(j == 1)
    def _():
        n = n_ref[...]
        v = v_ref[...]
        vc = v - jnp.mean(v, axis=-1, keepdims=True)
        rstd = lax.rsqrt(jnp.mean(jnp.square(vc), axis=-1, keepdims=True) + LN_EPS)
        vln = (vc * rstd * lng_ref[...] + lnb_ref[...]).astype(BF16)
        pi = lax.broadcasted_iota(jnp.int32, (SGU_BLOCK, SGU_BLOCK), 0) // CHUNK
        pj = lax.broadcasted_iota(jnp.int32, (SGU_BLOCK, SGU_BLOCK), 1) // CHUNK
        mask = (pj <= pi).astype(F32)
        for head in range(SGU_HEADS):
            cs = slice(head * SGU_HEAD_DIM, (head + 1) * SGU_HEAD_DIM)
            u = _gelu(_dot(n, w_ref[:, cs]) + b_ref[:, cs])
            w = (ws_ref[head] * mask).astype(BF16)
            bias = bs_ref[head]
            for blk in range(ROW_TILE // SGU_BLOCK):
                rs = slice(blk * SGU_BLOCK, (blk + 1) * SGU_BLOCK)
                sp = _dot(w, vln[rs, cs]) + bias
                a_ref[rs, cs] = (u[rs] * sp).astype(BF16)

    t0 = (i * ROW_TILE) % seq

    @pl.when((j == 2) & (t0 == 0))
    def _():
        z_ref[0:POOL_HALO, :] = jnp.zeros((POOL_HALO, D_MODEL), F32)

    @pl.when(j == 2)
    def _():
        n = n_ref[...]
        t = t0 + lax.broadcasted_iota(jnp.int32, (ROW_TILE, 1), 0)
        for k, win in reversed(list(enumerate(POOL_WINDOWS))):
            cs = slice(k * POOL_GROUP_DIM, (k + 1) * POOL_GROUP_DIM)
            z = _dot(n, w_ref[:, cs]) + b_ref[:, cs]
            z_ref[POOL_HALO:, cs] = z
            acc = z_ref[:, cs]
            d = 1
            while d < win:
                acc = acc + pltpu.roll(acc, d, axis=0)
                d *= 2
            inv_cnt = 1.0 / jnp.minimum(t + 1, win).astype(F32)
            pooled = (acc[POOL_HALO:] * inv_cnt - z).astype(BF16)
            mixed = _dot(pooled, pw_ref[k])
            p_ref[:, cs] = (mixed * ps_ref[:, cs]).astype(BF16)
        z_ref[0:POOL_HALO, :] = z_ref[ROW_TILE:ROW_TILE + POOL_HALO, :]


def _mix(h, norm_g, w_in, b_in, ln_g, ln_b, w_s, b_s, pool_w, pool_scale, *, seq):
    s = h.shape[0]
    row = lambda i, j: (i, 0)
    const = lambda i, j: (0, 0)
    const3 = lambda i, j: (0, 0, 0)
    wcol = lambda i, j: (0, jnp.where(j < 2, 1 - j, j))
    act = pl.BlockSpec((ROW_TILE, D_MODEL), row)
    return pl.pallas_call(
        functools.partial(_mix_kernel, seq=seq),
        grid=(s // ROW_TILE, 3),
        in_specs=[
            act,
            pl.BlockSpec((1, D_MODEL), const),
            pl.BlockSpec((D_MODEL, D_MODEL), wcol),
            pl.BlockSpec((1, D_MODEL), wcol),
            pl.BlockSpec((1, D_MODEL), const),
            pl.BlockSpec((1, D_MODEL), const),
            pl.BlockSpec((SGU_HEADS, SGU_BLOCK, SGU_BLOCK), const3),
            pl.BlockSpec((SGU_HEADS, SGU_BLOCK, 1), const3),
            pl.BlockSpec((len(POOL_WINDOWS), POOL_GROUP_DIM, POOL_GROUP_DIM), const3),
            pl.BlockSpec((1, D_MODEL), const),
        ],
        out_specs=[act, act, act],
        out_shape=[jax.ShapeDtypeStruct((s, D_MODEL), BF16)] * 3,
        scratch_shapes=[pltpu.VMEM((ROW_TILE, D_MODEL), F32),
                        pltpu.VMEM((ROW_TILE + POOL_HALO, D_MODEL), F32)],
        compiler_params=_params(("arbitrary", "arbitrary")),
        name="mix",
    )(h, norm_g, w_in, b_in, ln_g, ln_b, w_s, b_s, pool_w, pool_scale)


def _merge_kernel(h_ref, n_ref, a_ref, b_ref, wga_ref, wgb_ref, bga_ref, bgb_ref,
                  wa_ref, wb_ref, wo_ref, o_ref):
    j = pl.program_id(1)

    @pl.when(j == 0)
    def _():
        o_ref[...] = jnp.zeros_like(o_ref)

    col = pl.multiple_of(j * COL_TILE, COL_TILE)
    n = n_ref[...]
    gate_a = jax.nn.sigmoid(_dot(n, wga_ref[...]) + bga_ref[...])
    gate_b = jax.nn.sigmoid(_dot(n, wgb_ref[...]) + bgb_ref[...])
    y_a = _dot(a_ref[...], wa_ref[...])
    y_b = _dot(b_ref[...], wb_ref[...])
    merged = (gate_a * y_a + gate_b * y_b).astype(BF16)
    o_ref[...] += _dot(merged, wo_ref[pl.ds(col, COL_TILE), :])
    o_ref[:, pl.ds(col, COL_TILE)] += h_ref[...]


def _merge(h, n, a, b, w_in, b_in, w_a, w_b, w_o):
    s = h.shape[0]
    n_c = D_MODEL // COL_TILE
    ga0 = 3 * D_MODEL // COL_TILE
    gb0 = 4 * D_MODEL // COL_TILE
    row = lambda i, j: (i, 0)
    act = pl.BlockSpec((ROW_TILE, D_MODEL), row)
    colw = pl.BlockSpec((D_MODEL, COL_TILE), lambda i, j: (0, j))
    return pl.pallas_call(
        _merge_kernel,
        grid=(s // ROW_TILE, n_c),
        in_specs=[
            pl.BlockSpec((ROW_TILE, COL_TILE), lambda i, j: (i, j)),
            act, act, act,
            pl.BlockSpec((D_MODEL, COL_TILE), lambda i, j: (0, ga0 + j)),
            pl.BlockSpec((D_MODEL, COL_TILE), lambda i, j: (0, gb0 + j)),
            pl.BlockSpec((1, COL_TILE), lambda i, j: (0, ga0 + j)),
            pl.BlockSpec((1, COL_TILE), lambda i, j: (0, gb0 + j)),
            colw,
            colw,
            pl.BlockSpec((D_MODEL, D_MODEL), lambda i, j: (0, 0), pipeline_mode=pl.Buffered(1)),
        ],
        out_specs=act,
        out_shape=jax.ShapeDtypeStruct((s, D_MODEL), F32),
        compiler_params=_params(("arbitrary", "arbitrary")),
        name="merge",
    )(h, n, a, b, w_in, w_in, b_in, b_in, w_a, w_b, w_o)


def kernel(x, ffn1_norm, ffn1_w_in, ffn1_w_out, mix_norm, w_in, b_in, sgu_ln_g, sgu_ln_b, sgu_w_s, sgu_b_s, pool_w, pool_scale, w_branch_a, w_branch_b, w_out, ffn2_norm, ffn2_w_in, ffn2_w_out, final_norm):
    bsz, seq, d = x.shape
    assert d == D_MODEL and seq % ROW_TILE == 0 and (bsz * seq) % FFN_ROW_TILE == 0
    row2 = lambda v: v.reshape(1, -1)
    bf = lambda w: w.astype(BF16)

    h = x.reshape(bsz * seq, d)
    later_weights = (ffn2_w_in, ffn2_w_out, w_in, w_branch_a, w_branch_b, w_out)
    h, ffn2_w_in_bf, ffn2_w_out_bf, w_in_bf, w_a_bf, w_b_bf, w_o_bf = _ffn(
        h, row2(ffn1_norm), bf(ffn1_w_in), bf(ffn1_w_out), row2(final_norm),
        final_norm=False, cast_on_side=later_weights)
    b_in2 = row2(b_in)
    n, a, b = _mix(h, row2(mix_norm), w_in_bf, b_in2, row2(sgu_ln_g), row2(sgu_ln_b), sgu_w_s,
                   sgu_b_s[:, :, None], bf(pool_w), row2(pool_scale), seq=seq)
    h = _merge(h, n, a, b, w_in_bf, b_in2, w_a_bf, w_b_bf, w_o_bf)
    h, = _ffn(h, row2(ffn2_norm), ffn2_w_in_bf, ffn2_w_out_bf, row2(final_norm), final_norm=True)
    return h.reshape(bsz, seq, d).astype(x.dtype)
```

```python
import functools

import jax
import jax.numpy as jnp
from jax import lax
from jax.experimental import pallas as pl
from jax.experimental.pallas import tpu as pltpu

D_MODEL = 2048
D_FF = 5632
CHUNK = 64
SGU_BLOCK = 128
SGU_HEADS = 8
SGU_HEAD_DIM = D_MODEL // SGU_HEADS
POOL_WINDOWS = (2, 4, 8, 16)
POOL_GROUP_DIM = D_MODEL // len(POOL_WINDOWS)
POOL_HALO = 16
RMS_EPS = 1e-6
LN_EPS = 1e-5

ROW_TILE = 512
FFN_ROW_TILE = 1024
NORM_ROWS = 128
COL_TILE = 512
V7X_VMEM_LIMIT_BYTES = 60 * 1024 * 1024

F32 = jnp.float32
BF16 = jnp.bfloat16
LANES = 128
BF16_SUBLANES = 16


def _side_cast_specs(shape, n_tiles, n_steps):
    rows, cols = shape
    assert rows % n_tiles == 0 and (rows // n_tiles) % BF16_SUBLANES == 0
    n_col = max(c for c in range(1, n_steps + 1) if cols % c == 0 and (cols // c) % LANES == 0)
    index = lambda i, j: (i, jnp.minimum(j, n_col - 1))
    return pl.BlockSpec((rows // n_tiles, cols // n_col), index)


def _norm_tile(x_ref, g_ref, n_ref, rows, copy_ref=None):
    def step(r, carry):
        rs = pl.ds(pl.multiple_of(r * NORM_ROWS, NORM_ROWS), NORM_ROWS)
        x = x_ref[rs, :]
        rstd = lax.rsqrt(jnp.mean(x * x, axis=-1, keepdims=True) + RMS_EPS)
        n_ref[rs, :] = (x * rstd * g_ref[...]).astype(BF16)
        if copy_ref is not None:
            copy_ref[rs, :] = x
        return carry
    lax.fori_loop(0, rows // NORM_ROWS, step, None)


def _gelu(x):
    return 0.5 * x * (1.0 + lax.erf(x * (2.0 ** -0.5)))


def _dot(a, b):
    return jnp.dot(a, b, preferred_element_type=F32)


def _params(semantics):
    return pltpu.CompilerParams(dimension_semantics=semantics,
                                vmem_limit_bytes=V7X_VMEM_LIMIT_BYTES)


def _ffn_kernel(*refs, final_norm, n_side):
    x_ref, g_ref, wg_ref, wu_ref, wo_ref, fin_ref = refs[:6]
    side_in = refs[6:6 + n_side]
    o_ref = refs[6 + n_side]
    side_out = refs[7 + n_side:7 + 2 * n_side]
    xn_ref = refs[-1]
    j = pl.program_id(1)

    for src, dst in zip(side_in, side_out):
        dst[...] = src[...].astype(BF16)

    def chunk(xn):
        gate = _dot(xn, wg_ref[...])
        up = _dot(xn, wu_ref[...])
        act = (0.5 * gate * jax.nn.sigmoid(gate) * up).astype(BF16)
        return _dot(act, wo_ref[...])

    @pl.when(j == 0)
    def _():
        x = x_ref[...]
        xn = (x * lax.rsqrt(jnp.mean(x * x, axis=-1, keepdims=True) + RMS_EPS) * g_ref[...]).astype(BF16)
        xn_ref[...] = xn
        o_ref[...] = x + chunk(xn)

    @pl.when(j > 0)
    def _():
        o_ref[...] += chunk(xn_ref[...])

    if final_norm:
        @pl.when(j == pl.num_programs(1) - 1)
        def _():
            def step(r, carry):
                rs = pl.ds(pl.multiple_of(r * NORM_ROWS, NORM_ROWS), NORM_ROWS)
                h = o_ref[rs, :]
                o_ref[rs, :] = h * lax.rsqrt(jnp.mean(h * h, axis=-1, keepdims=True) + RMS_EPS) * fin_ref[...]
                return carry
            lax.fori_loop(0, FFN_ROW_TILE // NORM_ROWS, step, None)


def _ffn(x, norm_g, w_in, w_out, fin_g, *, final_norm, cast_on_side=()):
    s = x.shape[0]
    n_ff = D_FF // COL_TILE
    n_tiles = s // FFN_ROW_TILE
    row = lambda i, j: (i, 0)
    const = lambda i, j: (0, 0)
    side_specs = [_side_cast_specs(w.shape, n_tiles, n_ff) for w in cast_on_side]
    return pl.pallas_call(
        functools.partial(_ffn_kernel, final_norm=final_norm, n_side=len(cast_on_side)),
        grid=(n_tiles, n_ff),
        in_specs=[
            pl.BlockSpec((FFN_ROW_TILE, D_MODEL), row),
            pl.BlockSpec((1, D_MODEL), const),
            pl.BlockSpec((D_MODEL, COL_TILE), lambda i, j: (0, j)),
            pl.BlockSpec((D_MODEL, COL_TILE), lambda i, j: (0, j + n_ff)),
            pl.BlockSpec((COL_TILE, D_MODEL), lambda i, j: (j, 0)),
            pl.BlockSpec((1, D_MODEL), const),
        ] + side_specs,
        out_specs=[pl.BlockSpec((FFN_ROW_TILE, D_MODEL), row)] + side_specs,
        out_shape=[jax.ShapeDtypeStruct((s, D_MODEL), F32)]
                  + [jax.ShapeDtypeStruct(w.shape, BF16) for w in cast_on_side],
        scratch_shapes=[pltpu.VMEM((FFN_ROW_TILE, D_MODEL), BF16)],
        compiler_params=_params(("arbitrary", "arbitrary")),
        name="ffn_final" if final_norm else "ffn",
    )(x, norm_g, w_in, w_in, w_out, fin_g, *cast_on_side)


def _mix_kernel(h_ref, g_ref, w_ref, b_ref, lng_ref, lnb_ref, ws_ref, bs_ref, pw_ref, ps_ref,
                n_ref, a_ref, p_ref, v_ref, z_ref, *, seq):
    i, j = pl.program_id(0), pl.program_id(1)

    @pl.when(j == 0)
    def _():
        h = h_ref[...]
        n = (h * lax.rsqrt(jnp.mean(h * h, axis=-1, keepdims=True) + RMS_EPS) * g_ref[...]).astype(BF16)
        n_ref[...] = n
        v_ref[...] = _gelu(_dot(n, w_ref[...]) + b_ref[...])

    @pl.when(j == 1)
    def _():
        n = n_ref[...]
        v = v_ref[...]
        vc = v - jnp.mean(v, axis=-1, keepdims=True)
        rstd = lax.rsqrt(jnp.mean(jnp.square(vc), axis=-1, keepdims=True) + LN_EPS)
        vln = (vc * rstd * lng_ref[...] + lnb_ref[...]).astype(BF16)
        pi = lax.broadcasted_iota(jnp.int32, (SGU_BLOCK, SGU_BLOCK), 0) // CHUNK
        pj = lax.broadcasted_iota(jnp.int32, (SGU_BLOCK, SGU_BLOCK), 1) // CHUNK
        mask = (pj <= pi).astype(F32)
        for head in range(SGU_HEADS):
            cs = slice(head * SGU_HEAD_DIM, (head + 1) * SGU_HEAD_DIM)
            u = _gelu(_dot(n, w_ref[:, cs]) + b_ref[:, cs])
            w = (ws_ref[head] * mask).astype(BF16)
            bias = bs_ref[head]
            for blk in range(ROW_TILE // SGU_BLOCK):
                rs = slice(blk * SGU_BLOCK, (blk + 1) * SGU_BLOCK)
                sp = _dot(w, vln[rs, cs]) + bias
                a_ref[rs, cs] = (u[rs] * sp).astype(BF16)

    t0 = (i * ROW_TILE) % seq

    @pl.when((j == 2) & (t0 == 0))
    def _():
        z_ref[0:POOL_HALO, :] = jnp.zeros((POOL_HALO, D_MODEL), F32)

    @pl.when(j == 2)
    def _():
        n = n_ref[...]
        t = t0 + lax.broadcasted_iota(jnp.int32, (ROW_TILE, 1), 0)
        for k, win in reversed(list(enumerate(POOL_WINDOWS))):
            cs = slice(k * POOL_GROUP_DIM, (k + 1) * POOL_GROUP_DIM)
            z = _dot(n, w_ref[:, cs]) + b_ref[:, cs]
            z_ref[POOL_HALO:, cs] = z
            acc = z_ref[:, cs]
            d = 1
            while d < win:
                acc = acc + pltpu.roll(acc, d, axis=0)
                d *= 2
            inv_cnt = 1.0 / jnp.minimum(t + 1, win).astype(F32)
            pooled = (acc[POOL_HALO:] * inv_cnt - z).astype(BF16)
            mixed = _dot(pooled, pw_ref[k])
            p_ref[:, cs] = (mixed * ps_ref[:, cs]).astype(BF16)
        z_ref[0:POOL_HALO, :] = z_ref[ROW_TILE:ROW_TILE + POOL_HALO, :]


def _mix(h, norm_g, w_in, b_in, ln_g, ln_b, w_s, b_s, pool_w, pool_scale, *, seq):
    s = h.shape[0]
    row = lambda i, j: (i, 0)
    const = lambda i, j: (0, 0)
    const3 = lambda i, j: (0, 0, 0)
    wcol = lambda i, j: (0, jnp.where(j < 2, 1 - j, j))
    act = pl.BlockSpec((ROW_TILE, D_MODEL), row)
    return pl.pallas_call(
        functools.partial(_mix_kernel, seq=seq),
        grid=(s // ROW_TILE, 3),
        in_specs=[
            act,
            pl.BlockSpec((1, D_MODEL), const),
            pl.BlockSpec((D_MODEL, D_MODEL), wcol),
            pl.BlockSpec((1, D_MODEL), wcol),
            pl.BlockSpec((1, D_MODEL), const),
            pl.BlockSpec((1, D_MODEL), const),
            pl.BlockSpec((SGU_HEADS, SGU_BLOCK, SGU_BLOCK), const3),
            pl.BlockSpec((SGU_HEADS, SGU_BLOCK, 1), const3),
            pl.BlockSpec((len(POOL_WINDOWS), POOL_GROUP_DIM, POOL_GROUP_DIM), const3),
            pl.BlockSpec((1, D_MODEL), const),
        ],
        out_specs=[act, act, act],
        out_shape=[jax.ShapeDtypeStruct((s, D_MODEL), BF16)] * 3,
        scratch_shapes=[pltpu.VMEM((ROW_TILE, D_MODEL), F32),
                        pltpu.VMEM((ROW_TILE + POOL_HALO, D_MODEL), F32)],
        compiler_params=_params(("arbitrary", "arbitrary")),
        name="mix",
    )(h, norm_g, w_in, b_in, ln_g, ln_b, w_s, b_s, pool_w, pool_scale)


def _merge_kernel(h_ref, n_ref, a_ref, b_ref, wga_ref, wgb_ref, bga_ref, bgb_ref,
                  wa_ref, wb_ref, wo_ref, o_ref):
    j = pl.program_id(1)

    @pl.when(j == 0)
    def _():
        o_ref[...] = jnp.zeros_like(o_ref)

    col = pl.multiple_of(j * COL_TILE, COL_TILE)
    n = n_ref[...]
    gate_a = jax.nn.sigmoid(_dot(n, wga_ref[...]) + bga_ref[...])
    gate_b = jax.nn.sigmoid(_dot(n, wgb_ref[...]) + bgb_ref[...])
    y_a = _dot(a_ref[...], wa_ref[...])
    y_b = _dot(b_ref[...], wb_ref[...])
    merged = (gate_a * y_a + gate_b * y_b).astype(BF16)
    o_ref[...] += _dot(merged, wo_ref[pl.ds(col, COL_TILE), :])
    o_ref[:, pl.ds(col, COL_TILE)] += h_ref[...]


def _merge(h, n, a, b, w_in, b_in, w_a, w_b, w_o):
    s = h.shape[0]
    n_c = D_MODEL // COL_TILE
    ga0 = 3 * D_MODEL // COL_TILE
    gb0 = 4 * D_MODEL // COL_TILE
    row = lambda i, j: (i, 0)
    act = pl.BlockSpec((ROW_TILE, D_MODEL), row)
    colw = pl.BlockSpec((D_MODEL, COL_TILE), lambda i, j: (0, j))
    return pl.pallas_call(
        _merge_kernel,
        grid=(s // ROW_TILE, n_c),
        in_specs=[
            pl.BlockSpec((ROW_TILE, COL_TILE), lambda i, j: (i, j)),
            act, act, act,
            pl.BlockSpec((D_MODEL, COL_TILE), lambda i, j: (0, ga0 + j)),
            pl.BlockSpec((D_MODEL, COL_TILE), lambda i, j: (0, gb0 + j)),
            pl.BlockSpec((1, COL_TILE), lambda i, j: (0, ga0 + j)),
            pl.BlockSpec((1, COL_TILE), lambda i, j: (0, gb0 + j)),
            colw,
            colw,
            pl.BlockSpec((D_MODEL, D_MODEL), lambda i, j: (0, 0), pipeline_mode=pl.Buffered(1)),
        ],
        out_specs=act,
        out_shape=jax.ShapeDtypeStruct((s, D_MODEL), F32),
        compiler_params=_params(("arbitrary", "arbitrary")),
        name="merge",
    )(h, n, a, b, w_in, w_in, b_in, b_in, w_a, w_b, w_o)


def kernel(x, ffn1_norm, ffn1_w_in, ffn1_w_out, mix_norm, w_in, b_in, sgu_ln_g, sgu_ln_b, sgu_w_s, sgu_b_s, pool_w, pool_scale, w_branch_a, w_branch_b, w_out, ffn2_norm, ffn2_w_in, ffn2_w_out, final_norm):
    bsz, seq, d = x.shape
    assert d == D_MODEL and seq % ROW_TILE == 0 and (bsz * seq) % FFN_ROW_TILE == 0
    row2 = lambda v: v.reshape(1, -1)
    bf = lambda w: w.astype(BF16)

    h = x.reshape(bsz * seq, d)
    later_weights = (ffn2_w_in, ffn2_w_out, w_in, w_branch_a, w_branch_b, w_out)
    h, ffn2_w_in_bf, ffn2_w_out_bf, w_in_bf, w_a_bf, w_b_bf, w_o_bf = _ffn(
        h, row2(ffn1_norm), bf(ffn1_w_in), bf(ffn1_w_out), row2(final_norm),
        final_norm=False, cast_on_side=later_weights)
    b_in2 = row2(b_in)
    n, a, b = _mix(h, row2(mix_norm), w_in_bf, b_in2, row2(sgu_ln_g), row2(sgu_ln_b), sgu_w_s,
                   sgu_b_s[:, :, None], bf(pool_w), row2(pool_scale), seq=seq)
    h = _merge(h, n, a, b, w_in_bf, b_in2, w_a_bf, w_b_bf, w_o_bf)
    h, = _ffn(h, row2(ffn2_norm), ffn2_w_in_bf, ffn2_w_out_bf, row2(final_norm), final_norm=True)
    return h.reshape(bsz, seq, d).astype(x.dtype)
```

```python
import functools

import jax
import jax.numpy as jnp
from jax import lax
from jax.experimental import pallas as pl
from jax.experimental.pallas import tpu as pltpu

D_MODEL = 2048
D_FF = 5632
CHUNK = 64
SGU_BLOCK = 128
SGU_HEADS = 8
SGU_HEAD_DIM = D_MODEL // SGU_HEADS
POOL_WINDOWS = (2, 4, 8, 16)
POOL_GROUP_DIM = D_MODEL // len(POOL_WINDOWS)
POOL_HALO = 16
RMS_EPS = 1e-6
LN_EPS = 1e-5

ROW_TILE = 512
FFN_ROW_TILE = 1024
NORM_ROWS = 128
COL_TILE = 512
V7X_VMEM_LIMIT_BYTES = 60 * 1024 * 1024

F32 = jnp.float32
BF16 = jnp.bfloat16
LANES = 128
BF16_SUBLANES = 16


def _side_cast_specs(shape, n_tiles, n_steps):
    rows, cols = shape
    assert rows % n_tiles == 0 and (rows // n_tiles) % BF16_SUBLANES == 0
    n_col = max(c for c in range(1, n_steps + 1) if cols % c == 0 and (cols // c) % LANES == 0)
    index = lambda i, j: (i, jnp.minimum(j, n_col - 1))
    return pl.BlockSpec((rows // n_tiles, cols // n_col), index)


def _norm_tile(x_ref, g_ref, n_ref, rows, copy_ref=None):
    def step(r, carry):
        rs = pl.ds(pl.multiple_of(r * NORM_ROWS, NORM_ROWS), NORM_ROWS)
        x = x_ref[rs, :]
        rstd = lax.rsqrt(jnp.mean(x * x, axis=-1, keepdims=True) + RMS_EPS)
        n_ref[rs, :] = (x * rstd * g_ref[...]).astype(BF16)
        if copy_ref is not None:
            copy_ref[rs, :] = x
        return carry
    lax.fori_loop(0, rows // NORM_ROWS, step, None)


def _gelu(x):
    return 0.5 * x * (1.0 + lax.erf(x * (2.0 ** -0.5)))


def _dot(a, b):
    return jnp.dot(a, b, preferred_element_type=F32)


def _params(semantics):
    return pltpu.CompilerParams(dimension_semantics=semantics,
                                vmem_limit_bytes=V7X_VMEM_LIMIT_BYTES)


def _ffn_kernel(*refs, final_norm, n_side):
    x_ref, g_ref, wg_ref, wu_ref, wo_ref, fin_ref = refs[:6]
    side_in = refs[6:6 + n_side]
    o_ref = refs[6 + n_side]
    side_out = refs[7 + n_side:7 + 2 * n_side]
    xn_ref = refs[-1]
    j = pl.program_id(1)

    for src, dst in zip(side_in, side_out):
        dst[...] = src[...].astype(BF16)

    def chunk(xn):
        gate = _dot(xn, wg_ref[...])
        up = _dot(xn, wu_ref[...])
        act = (0.5 * gate * jax.nn.sigmoid(gate) * up).astype(BF16)
        return _dot(act, wo_ref[...])

    @pl.when(j == 0)
    def _():
        x = x_ref[...]
        xn = (x * lax.rsqrt(jnp.mean(x * x, axis=-1, keepdims=True) + RMS_EPS) * g_ref[...]).astype(BF16)
        xn_ref[...] = xn
        o_ref[...] = x + chunk(xn)

    @pl.when(j > 0)
    def _():
        o_ref[...] += chunk(xn_ref[...])

    if final_norm:
        @pl.when(j == pl.num_programs(1) - 1)
        def _():
            def step(r, carry):
                rs = pl.ds(pl.multiple_of(r * NORM_ROWS, NORM_ROWS), NORM_ROWS)
                h = o_ref[rs, :]
                o_ref[rs, :] = h * lax.rsqrt(jnp.mean(h * h, axis=-1, keepdims=True) + RMS_EPS) * fin_ref[...]
                return carry
            lax.fori_loop(0, FFN_ROW_TILE // NORM_ROWS, step, None)


def _ffn(x, norm_g, w_in, w_out, fin_g, *, final_norm, cast_on_side=()):
    s = x.shape[0]
    n_ff = D_FF // COL_TILE
    n_tiles = s // FFN_ROW_TILE
    row = lambda i, j: (i, 0)
    const = lambda i, j: (0, 0)
    next_row_early = lambda i, j: (jnp.minimum(i + jnp.where(j > n_ff // 2, 1, 0), n_tiles - 1), 0)
    side_specs = [_side_cast_specs(w.shape, n_tiles, n_ff) for w in cast_on_side]
    return pl.pallas_call(
        functools.partial(_ffn_kernel, final_norm=final_norm, n_side=len(cast_on_side)),
        grid=(n_tiles, n_ff),
        in_specs=[
            pl.BlockSpec((FFN_ROW_TILE, D_MODEL), next_row_early),
            pl.BlockSpec((1, D_MODEL), const),
            pl.BlockSpec((D_MODEL, COL_TILE), lambda i, j: (0, j)),
            pl.BlockSpec((D_MODEL, COL_TILE), lambda i, j: (0, j + n_ff)),
            pl.BlockSpec((COL_TILE, D_MODEL), lambda i, j: (j, 0)),
            pl.BlockSpec((1, D_MODEL), const),
        ] + side_specs,
        out_specs=[pl.BlockSpec((FFN_ROW_TILE, D_MODEL), row)] + side_specs,
        out_shape=[jax.ShapeDtypeStruct((s, D_MODEL), F32)]
                  + [jax.ShapeDtypeStruct(w.shape, BF16) for w in cast_on_side],
        scratch_shapes=[pltpu.VMEM((FFN_ROW_TILE, D_MODEL), BF16)],
        compiler_params=_params(("arbitrary", "arbitrary")),
        name="ffn_final" if final_norm else "ffn",
    )(x, norm_g, w_in, w_in, w_out, fin_g, *cast_on_side)


def _mix_kernel(h_ref, g_ref, w_ref, b_ref, lng_ref, lnb_ref, ws_ref, bs_ref, pw_ref, ps_ref,
                n_ref, a_ref, p_ref, v_ref, z_ref, *, seq):
    i, j = pl.program_id(0), pl.program_id(1)

    @pl.when(j == 0)
    def _():
        h = h_ref[...]
        n = (h * lax.rsqrt(jnp.mean(h * h, axis=-1, keepdims=True) + RMS_EPS) * g_ref[...]).astype(BF16)
        n_ref[...] = n
        v_ref[...] = _gelu(_dot(n, w_ref[...]) + b_ref[...])

    @pl.when(j == 1)
    def _():
        n = n_ref[...]
        v = v_ref[...]
        vc = v - jnp.mean(v, axis=-1, keepdims=True)
        rstd = lax.rsqrt(jnp.mean(jnp.square(vc), axis=-1, keepdims=True) + LN_EPS)
        vln = (vc * rstd * lng_ref[...] + lnb_ref[...]).astype(BF16)
        pi = lax.broadcasted_iota(jnp.int32, (SGU_BLOCK, SGU_BLOCK), 0) // CHUNK
        pj = lax.broadcasted_iota(jnp.int32, (SGU_BLOCK, SGU_BLOCK), 1) // CHUNK
        mask = (pj <= pi).astype(F32)
        for head in range(SGU_HEADS):
            cs = slice(head * SGU_HEAD_DIM, (head + 1) * SGU_HEAD_DIM)
            u = _gelu(_dot(n, w_ref[:, cs]) + b_ref[:, cs])
            w = (ws_ref[head] * mask).astype(BF16)
            bias = bs_ref[head]
            for blk in range(ROW_TILE // SGU_BLOCK):
                rs = slice(blk * SGU_BLOCK, (blk + 1) * SGU_BLOCK)
                sp = _dot(w, vln[rs, cs]) + bias
                a_ref[rs, cs] = (u[rs] * sp).astype(BF16)

    t0 = (i * ROW_TILE) % seq

    @pl.when((j == 2) & (t0 == 0))
    def _():
        z_ref[0:POOL_HALO, :] = jnp.zeros((POOL_HALO, D_MODEL), F32)

    @pl.when(j == 2)
    def _():
        n = n_ref[...]
        t = t0 + lax.broadcasted_iota(jnp.int32, (ROW_TILE, 1), 0)
        for k, win in reversed(list(enumerate(POOL_WINDOWS))):
            cs = slice(k * POOL_GROUP_DIM, (k + 1) * POOL_GROUP_DIM)
            z = _dot(n, w_ref[:, cs]) + b_ref[:, cs]
            z_ref[POOL_HALO:, cs] = z
            acc = z_ref[:, cs]
            d = 1
            while d < win:
                acc = acc + pltpu.roll(acc, d, axis=0)
                d *= 2
            inv_cnt = 1.0 / jnp.minimum(t + 1, win).astype(F32)
            pooled = (acc[POOL_HALO:] * inv_cnt - z).astype(BF16)
            mixed = _dot(pooled, pw_ref[k])
            p_ref[:, cs] = (mixed * ps_ref[:, cs]).astype(BF16)
        z_ref[0:POOL_HALO, :] = z_ref[ROW_TILE:ROW_TILE + POOL_HALO, :]


def _mix(h, norm_g, w_in, b_in, ln_g, ln_b, w_s, b_s, pool_w, pool_scale, *, seq):
    s = h.shape[0]
    row = lambda i, j: (i, 0)
    const = lambda i, j: (0, 0)
    const3 = lambda i, j: (0, 0, 0)
    wcol = lambda i, j: (0, jnp.where(j < 2, 1 - j, j))
    act = pl.BlockSpec((ROW_TILE, D_MODEL), row)
    n_tiles = s // ROW_TILE
    return pl.pallas_call(
        functools.partial(_mix_kernel, seq=seq),
        grid=(n_tiles, 3),
        in_specs=[
            pl.BlockSpec((ROW_TILE, D_MODEL),
                         lambda i, j: (jnp.minimum(i + jnp.where(j > 0, 1, 0), n_tiles - 1), 0)),
            pl.BlockSpec((1, D_MODEL), const),
            pl.BlockSpec((D_MODEL, D_MODEL), wcol),
            pl.BlockSpec((1, D_MODEL), wcol),
            pl.BlockSpec((1, D_MODEL), const),
            pl.BlockSpec((1, D_MODEL), const),
            pl.BlockSpec((SGU_HEADS, SGU_BLOCK, SGU_BLOCK), const3),
            pl.BlockSpec((SGU_HEADS, SGU_BLOCK, 1), const3),
            pl.BlockSpec((len(POOL_WINDOWS), POOL_GROUP_DIM, POOL_GROUP_DIM), const3),
            pl.BlockSpec((1, D_MODEL), const),
        ],
        out_specs=[act, act, act],
        out_shape=[jax.ShapeDtypeStruct((s, D_MODEL), BF16)] * 3,
        scratch_shapes=[pltpu.VMEM((ROW_TILE, D_MODEL), F32),
                        pltpu.VMEM((ROW_TILE + POOL_HALO, D_MODEL), F32)],
        compiler_params=_params(("arbitrary", "arbitrary")),
        name="mix",
    )(h, norm_g, w_in, b_in, ln_g, ln_b, w_s, b_s, pool_w, pool_scale)


def _merge_kernel(h_ref, n_ref, a_ref, b_ref, wga_ref, wgb_ref, bga_ref, bgb_ref,
                  wa_ref, wb_ref, wo_ref, o_ref):
    j = pl.program_id(1)

    @pl.when(j == 0)
    def _():
        o_ref[...] = jnp.zeros_like(o_ref)

    col = pl.multiple_of(j * COL_TILE, COL_TILE)
    n = n_ref[...]
    gate_a = jax.nn.sigmoid(_dot(n, wga_ref[...]) + bga_ref[...])
    gate_b = jax.nn.sigmoid(_dot(n, wgb_ref[...]) + bgb_ref[...])
    y_a = _dot(a_ref[...], wa_ref[...])
    y_b = _dot(b_ref[...], wb_ref[...])
    merged = (gate_a * y_a + gate_b * y_b).astype(BF16)
    o_ref[...] += _dot(merged, wo_ref[pl.ds(col, COL_TILE), :])
    o_ref[:, pl.ds(col, COL_TILE)] += h_ref[...]


def _merge(h, n, a, b, w_in, b_in, w_a, w_b, w_o):
    s = h.shape[0]
    n_c = D_MODEL // COL_TILE
    ga0 = 3 * D_MODEL // COL_TILE
    gb0 = 4 * D_MODEL // COL_TILE
    row = lambda i, j: (i, 0)
    act = pl.BlockSpec((ROW_TILE, D_MODEL), row)
    colw = pl.BlockSpec((D_MODEL, COL_TILE), lambda i, j: (0, j))
    return pl.pallas_call(
        _merge_kernel,
        grid=(s // ROW_TILE, n_c),
        in_specs=[
            pl.BlockSpec((ROW_TILE, COL_TILE), lambda i, j: (i, j)),
            act, act, act,
            pl.BlockSpec((D_MODEL, COL_TILE), lambda i, j: (0, ga0 + j)),
            pl.BlockSpec((D_MODEL, COL_TILE), lambda i, j: (0, gb0 + j)),
            pl.BlockSpec((1, COL_TILE), lambda i, j: (0, ga0 + j)),
            pl.BlockSpec((1, COL_TILE), lambda i, j: (0, gb0 + j)),
            colw,
            colw,
            pl.BlockSpec((D_MODEL, D_MODEL), lambda i, j: (0, 0), pipeline_mode=pl.Buffered(1)),
        ],
        out_specs=act,
        out_shape=jax.ShapeDtypeStruct((s, D_MODEL), F32),
        compiler_params=_params(("arbitrary", "arbitrary")),
        name="merge",
    )(h, n, a, b, w_in, w_in, b_in, b_in, w_a, w_b, w_o)


def kernel(x, ffn1_norm, ffn1_w_in, ffn1_w_out, mix_norm, w_in, b_in, sgu_ln_g, sgu_ln_b, sgu_w_s, sgu_b_s, pool_w, pool_scale, w_branch_a, w_branch_b, w_out, ffn2_norm, ffn2_w_in, ffn2_w_out, final_norm):
    bsz, seq, d = x.shape
    assert d == D_MODEL and seq % ROW_TILE == 0 and (bsz * seq) % FFN_ROW_TILE == 0
    row2 = lambda v: v.reshape(1, -1)
    bf = lambda w: w.astype(BF16)

    h = x.reshape(bsz * seq, d)
    later_weights = (ffn2_w_in, ffn2_w_out, w_in, w_branch_a, w_branch_b, w_out)
    h, ffn2_w_in_bf, ffn2_w_out_bf, w_in_bf, w_a_bf, w_b_bf, w_o_bf = _ffn(
        h, row2(ffn1_norm), bf(ffn1_w_in), bf(ffn1_w_out), row2(final_norm),
        final_norm=False, cast_on_side=later_weights)
    b_in2 = row2(b_in)
    n, a, b = _mix(h, row2(mix_norm), w_in_bf, b_in2, row2(sgu_ln_g), row2(sgu_ln_b), sgu_w_s,
                   sgu_b_s[:, :, None], bf(pool_w), row2(pool_scale), seq=seq)
    h = _merge(h, n, a, b, w_in_bf, b_in2, w_a_bf, w_b_bf, w_o_bf)
    h, = _ffn(h, row2(ffn2_norm), ffn2_w_in_bf, ffn2_w_out_bf, row2(final_norm), final_norm=True)
    return h.reshape(bsz, seq, d).astype(x.dtype)
```

```python
import functools

import jax
import jax.numpy as jnp
from jax import lax
from jax.experimental import pallas as pl
from jax.experimental.pallas import tpu as pltpu

D_MODEL = 2048
D_FF = 5632
CHUNK = 64
SGU_BLOCK = 128
SGU_HEADS = 8
SGU_HEAD_DIM = D_MODEL // SGU_HEADS
POOL_WINDOWS = (2, 4, 8, 16)
POOL_GROUP_DIM = D_MODEL // len(POOL_WINDOWS)
POOL_HALO = 16
RMS_EPS = 1e-6
LN_EPS = 1e-5

ROW_TILE = 512
FFN_ROW_TILE = 1024
NORM_ROWS = 128
COL_TILE = 512
HEAD_COL_TILE = 256
V7X_VMEM_LIMIT_BYTES = 60 * 1024 * 1024

F32 = jnp.float32
BF16 = jnp.bfloat16
LANES = 128
BF16_SUBLANES = 16


def _side_cast_specs(shape, n_tiles, n_steps):
    rows, cols = shape
    assert rows % n_tiles == 0 and (rows // n_tiles) % BF16_SUBLANES == 0
    n_col = max(c for c in range(1, n_steps + 1) if cols % c == 0 and (cols // c) % LANES == 0)
    index = lambda i, j: (i, jnp.minimum(j, n_col - 1))
    return pl.BlockSpec((rows // n_tiles, cols // n_col), index)


def _norm_tile(x_ref, g_ref, n_ref, rows, copy_ref=None):
    def step(r, carry):
        rs = pl.ds(pl.multiple_of(r * NORM_ROWS, NORM_ROWS), NORM_ROWS)
        x = x_ref[rs, :]
        rstd = lax.rsqrt(jnp.mean(x * x, axis=-1, keepdims=True) + RMS_EPS)
        n_ref[rs, :] = (x * rstd * g_ref[...]).astype(BF16)
        if copy_ref is not None:
            copy_ref[rs, :] = x
        return carry
    lax.fori_loop(0, rows // NORM_ROWS, step, None)


def _gelu(x):
    return 0.5 * x * (1.0 + lax.erf(x * (2.0 ** -0.5)))


def _dot(a, b):
    return jnp.dot(a, b, preferred_element_type=F32)


def _params(semantics):
    return pltpu.CompilerParams(dimension_semantics=semantics,
                                vmem_limit_bytes=V7X_VMEM_LIMIT_BYTES)


def _ffn_kernel(*refs, final_norm, n_side, weights_f32, skip_first_tile):
    n_emit = 3 if weights_f32 else 0
    x_ref, g_ref, wg_ref, wu_ref, wo_ref, fin_ref = refs[:6]
    side_in = refs[6:6 + n_side]
    o_ref = refs[6 + n_side]
    emit_out = refs[7 + n_side:7 + n_side + n_emit]
    side_out = refs[7 + n_side + n_emit:7 + 2 * n_side + n_emit]
    xn_ref = refs[-1]
    i, j = pl.program_id(0), pl.program_id(1)

    for src, dst in zip(side_in, side_out):
        dst[...] = src[...].astype(BF16)

    def chunk(xn):
        wg, wu, wo = wg_ref[...], wu_ref[...], wo_ref[...]
        if weights_f32:
            wg, wu, wo = wg.astype(BF16), wu.astype(BF16), wo.astype(BF16)
            for dst, w in zip(emit_out, (wg, wu, wo)):
                dst[...] = w
        gate = _dot(xn, wg)
        up = _dot(xn, wu)
        act = (0.5 * gate * jax.nn.sigmoid(gate) * up).astype(BF16)
        return _dot(act, wo)

    active = (i > 0) if skip_first_tile else True

    @pl.when((j == 0) & active)
    def _():
        x = x_ref[...]
        xn = (x * lax.rsqrt(jnp.mean(x * x, axis=-1, keepdims=True) + RMS_EPS) * g_ref[...]).astype(BF16)
        xn_ref[...] = xn
        o_ref[...] = x + chunk(xn)

    @pl.when((j > 0) & active)
    def _():
        o_ref[...] += chunk(xn_ref[...])

    if skip_first_tile:
        @pl.when((i == 0) & (j == 0))
        def _():
            o_ref[...] = jnp.zeros_like(o_ref)

    if final_norm:
        @pl.when(j == pl.num_programs(1) - 1)
        def _():
            def step(r, carry):
                rs = pl.ds(pl.multiple_of(r * NORM_ROWS, NORM_ROWS), NORM_ROWS)
                h = o_ref[rs, :]
                o_ref[rs, :] = h * lax.rsqrt(jnp.mean(h * h, axis=-1, keepdims=True) + RMS_EPS) * fin_ref[...]
                return carry
            lax.fori_loop(0, FFN_ROW_TILE // NORM_ROWS, step, None)


def _ffn(x, norm_g, w_gate, w_up, up_block, w_out, fin_g, *, final_norm, cast_on_side=(),
         skip_first_tile=False):
    s = x.shape[0]
    n_ff = D_FF // COL_TILE
    n_tiles = s // FFN_ROW_TILE
    row = lambda i, j: (i, 0)
    const = lambda i, j: (0, 0)
    step = (lambda i, j: jnp.where(i == 0, 0, j)) if skip_first_tile else (lambda i, j: j)
    side_specs = [_side_cast_specs(w.shape, n_tiles, n_ff) for w in cast_on_side]
    return pl.pallas_call(
        functools.partial(_ffn_kernel, final_norm=final_norm, n_side=len(cast_on_side),
                          weights_f32=False, skip_first_tile=skip_first_tile),
        grid=(n_tiles, n_ff),
        in_specs=[
            pl.BlockSpec((FFN_ROW_TILE, D_MODEL), row),
            pl.BlockSpec((1, D_MODEL), const),
            pl.BlockSpec((D_MODEL, COL_TILE), lambda i, j: (0, step(i, j))),
            pl.BlockSpec((D_MODEL, COL_TILE), lambda i, j: (0, step(i, j) + up_block)),
            pl.BlockSpec((COL_TILE, D_MODEL), lambda i, j: (step(i, j), 0)),
            pl.BlockSpec((1, D_MODEL), const),
        ] + side_specs,
        out_specs=[pl.BlockSpec((FFN_ROW_TILE, D_MODEL), row)] + side_specs,
        out_shape=[jax.ShapeDtypeStruct((s, D_MODEL), F32)]
                  + [jax.ShapeDtypeStruct(w.shape, BF16) for w in cast_on_side],
        scratch_shapes=[pltpu.VMEM((FFN_ROW_TILE, D_MODEL), BF16)],
        compiler_params=_params(("arbitrary", "arbitrary")),
        name="ffn_final" if final_norm else "ffn",
    )(x, norm_g, w_gate, w_up, w_out, fin_g, *cast_on_side)


def _ffn_head(x, norm_g, w_in, w_out, fin_g):
    n_ff = D_FF // HEAD_COL_TILE
    const = lambda i, j: (0, 0)
    gate = pl.BlockSpec((D_MODEL, HEAD_COL_TILE), lambda i, j: (0, j))
    down = pl.BlockSpec((HEAD_COL_TILE, D_MODEL), lambda i, j: (j, 0))
    tile = pl.BlockSpec((FFN_ROW_TILE, D_MODEL), const)
    return pl.pallas_call(
        functools.partial(_ffn_kernel, final_norm=False, n_side=0, weights_f32=True,
                          skip_first_tile=False),
        grid=(1, n_ff),
        in_specs=[
            pl.BlockSpec((FFN_ROW_TILE, D_MODEL), const, pipeline_mode=pl.Buffered(1)),
            pl.BlockSpec((1, D_MODEL), const),
            gate,
            pl.BlockSpec((D_MODEL, HEAD_COL_TILE), lambda i, j: (0, j + n_ff)),
            down,
            pl.BlockSpec((1, D_MODEL), const),
        ],
        out_specs=[tile, gate, gate, down],
        out_shape=[jax.ShapeDtypeStruct((FFN_ROW_TILE, D_MODEL), F32),
                   jax.ShapeDtypeStruct((D_MODEL, D_FF), BF16),
                   jax.ShapeDtypeStruct((D_MODEL, D_FF), BF16),
                   jax.ShapeDtypeStruct((D_FF, D_MODEL), BF16)],
        scratch_shapes=[pltpu.VMEM((FFN_ROW_TILE, D_MODEL), BF16)],
        compiler_params=_params(("arbitrary", "arbitrary")),
        name="ffn_head",
    )(x, norm_g, w_in, w_in, w_out, fin_g)


def _mix_kernel(h_ref, g_ref, w_ref, b_ref, lng_ref, lnb_ref, ws_ref, bs_ref, pw_ref, ps_ref,
                n_ref, a_ref, p_ref, v_ref, z_ref, *, seq):
    i, j = pl.program_id(0), pl.program_id(1)

    @pl.when(j == 0)
    def _():
        h = h_ref[...]
        n = (h * lax.rsqrt(jnp.mean(h * h, axis=-1, keepdims=True) + RMS_EPS) * g_ref[...]).astype(BF16)
        n_ref[...] = n
        v_ref[...] = _gelu(_dot(n, w_ref[...]) + b_ref[...])

    @pl.when(j == 1)
    def _():
        n = n_ref[...]
        v = v_ref[...]
        vc = v - jnp.mean(v, axis=-1, keepdims=True)
        rstd = lax.rsqrt(jnp.mean(jnp.square(vc), axis=-1, keepdims=True) + LN_EPS)
        vln = (vc * rstd * lng_ref[...] + lnb_ref[...]).astype(BF16)
        pi = lax.broadcasted_iota(jnp.int32, (SGU_BLOCK, SGU_BLOCK), 0) // CHUNK
        pj = lax.broadcasted_iota(jnp.int32, (SGU_BLOCK, SGU_BLOCK), 1) // CHUNK
        mask = (pj <= pi).astype(F32)
        for head in range(SGU_HEADS):
            cs = slice(head * SGU_HEAD_DIM, (head + 1) * SGU_HEAD_DIM)
            u = _gelu(_dot(n, w_ref[:, cs]) + b_ref[:, cs])
            w = (ws_ref[head] * mask).astype(BF16)
            bias = bs_ref[head]
            for blk in range(ROW_TILE // SGU_BLOCK):
                rs = slice(blk * SGU_BLOCK, (blk + 1) * SGU_BLOCK)
                sp = _dot(w, vln[rs, cs]) + bias
                a_ref[rs, cs] = (u[rs] * sp).astype(BF16)

    t0 = (i * ROW_TILE) % seq

    @pl.when((j == 2) & (t0 == 0))
    def _():
        z_ref[0:POOL_HALO, :] = jnp.zeros((POOL_HALO, D_MODEL), F32)

    @pl.when(j == 2)
    def _():
        n = n_ref[...]
        t = t0 + lax.broadcasted_iota(jnp.int32, (ROW_TILE, 1), 0)
        for k, win in reversed(list(enumerate(POOL_WINDOWS))):
            cs = slice(k * POOL_GROUP_DIM, (k + 1) * POOL_GROUP_DIM)
            z = _dot(n, w_ref[:, cs]) + b_ref[:, cs]
            z_ref[POOL_HALO:, cs] = z
            acc = z_ref[:, cs]
            d = 1
            while d < win:
                acc = acc + pltpu.roll(acc, d, axis=0)
                d *= 2
            inv_cnt = 1.0 / jnp.minimum(t + 1, win).astype(F32)
            pooled = (acc[POOL_HALO:] * inv_cnt - z).astype(BF16)
            mixed = _dot(pooled, pw_ref[k])
            p_ref[:, cs] = (mixed * ps_ref[:, cs]).astype(BF16)
        z_ref[0:POOL_HALO, :] = z_ref[ROW_TILE:ROW_TILE + POOL_HALO, :]


def _mix(h, norm_g, w_in, b_in, ln_g, ln_b, w_s, b_s, pool_w, pool_scale, *, seq):
    s = h.shape[0]
    row = lambda i, j: (i, 0)
    const = lambda i, j: (0, 0)
    const3 = lambda i, j: (0, 0, 0)
    wcol = lambda i, j: (0, jnp.where(j < 2, 1 - j, j))
    act = pl.BlockSpec((ROW_TILE, D_MODEL), row)
    return pl.pallas_call(
        functools.partial(_mix_kernel, seq=seq),
        grid=(s // ROW_TILE, 3),
        in_specs=[
            act,
            pl.BlockSpec((1, D_MODEL), const),
            pl.BlockSpec((D_MODEL, D_MODEL), wcol),
            pl.BlockSpec((1, D_MODEL), wcol),
            pl.BlockSpec((1, D_MODEL), const),
            pl.BlockSpec((1, D_MODEL), const),
            pl.BlockSpec((SGU_HEADS, SGU_BLOCK, SGU_BLOCK), const3),
            pl.BlockSpec((SGU_HEADS, SGU_BLOCK, 1), const3),
            pl.BlockSpec((len(POOL_WINDOWS), POOL_GROUP_DIM, POOL_GROUP_DIM), const3),
            pl.BlockSpec((1, D_MODEL), const),
        ],
        out_specs=[act, act, act],
        out_shape=[jax.ShapeDtypeStruct((s, D_MODEL), BF16)] * 3,
        scratch_shapes=[pltpu.VMEM((ROW_TILE, D_MODEL), F32),
                        pltpu.VMEM((ROW_TILE + POOL_HALO, D_MODEL), F32)],
        compiler_params=_params(("arbitrary", "arbitrary")),
        name="mix",
    )(h, norm_g, w_in, b_in, ln_g, ln_b, w_s, b_s, pool_w, pool_scale)


def _merge_kernel(h_ref, n_ref, a_ref, b_ref, wga_ref, wgb_ref, bga_ref, bgb_ref,
                  wa_ref, wb_ref, wo_ref, o_ref):
    j = pl.program_id(1)

    @pl.when(j == 0)
    def _():
        o_ref[...] = jnp.zeros_like(o_ref)

    col = pl.multiple_of(j * COL_TILE, COL_TILE)
    n = n_ref[...]
    gate_a = jax.nn.sigmoid(_dot(n, wga_ref[...]) + bga_ref[...])
    gate_b = jax.nn.sigmoid(_dot(n, wgb_ref[...]) + bgb_ref[...])
    y_a = _dot(a_ref[...], wa_ref[...])
    y_b = _dot(b_ref[...], wb_ref[...])
    merged = (gate_a * y_a + gate_b * y_b).astype(BF16)
    o_ref[...] += _dot(merged, wo_ref[pl.ds(col, COL_TILE), :])
    o_ref[:, pl.ds(col, COL_TILE)] += h_ref[...]


def _merge(h, n, a, b, w_in, b_in, w_a, w_b, w_o):
    s = h.shape[0]
    n_c = D_MODEL // COL_TILE
    ga0 = 3 * D_MODEL // COL_TILE
    gb0 = 4 * D_MODEL // COL_TILE
    row = lambda i, j: (i, 0)
    act = pl.BlockSpec((ROW_TILE, D_MODEL), row)
    colw = pl.BlockSpec((D_MODEL, COL_TILE), lambda i, j: (0, j))
    return pl.pallas_call(
        _merge_kernel,
        grid=(s // ROW_TILE, n_c),
        in_specs=[
            pl.BlockSpec((ROW_TILE, COL_TILE), lambda i, j: (i, j)),
            act, act, act,
            pl.BlockSpec((D_MODEL, COL_TILE), lambda i, j: (0, ga0 + j)),
            pl.BlockSpec((D_MODEL, COL_TILE), lambda i, j: (0, gb0 + j)),
            pl.BlockSpec((1, COL_TILE), lambda i, j: (0, ga0 + j)),
            pl.BlockSpec((1, COL_TILE), lambda i, j: (0, gb0 + j)),
            colw,
            colw,
            pl.BlockSpec((D_MODEL, D_MODEL), lambda i, j: (0, 0), pipeline_mode=pl.Buffered(1)),
        ],
        out_specs=act,
        out_shape=jax.ShapeDtypeStruct((s, D_MODEL), F32),
        compiler_params=_params(("arbitrary", "arbitrary")),
        name="merge",
    )(h, n, a, b, w_in, w_in, b_in, b_in, w_a, w_b, w_o)


def kernel(x, ffn1_norm, ffn1_w_in, ffn1_w_out, mix_norm, w_in, b_in, sgu_ln_g, sgu_ln_b, sgu_w_s, sgu_b_s, pool_w, pool_scale, w_branch_a, w_branch_b, w_out, ffn2_norm, ffn2_w_in, ffn2_w_out, final_norm):
    bsz, seq, d = x.shape
    assert d == D_MODEL and seq % ROW_TILE == 0 and (bsz * seq) % FFN_ROW_TILE == 0
    row2 = lambda v: v.reshape(1, -1)
    bf = lambda w: w.astype(BF16)

    h = x.reshape(bsz * seq, d)
    later_weights = (ffn2_w_in, ffn2_w_out, w_in, w_branch_a, w_branch_b, w_out)
    head, w1g_bf, w1u_bf, w1o_bf = _ffn_head(h, row2(ffn1_norm), ffn1_w_in, ffn1_w_out, row2(final_norm))
    h, ffn2_w_in_bf, ffn2_w_out_bf, w_in_bf, w_a_bf, w_b_bf, w_o_bf = _ffn(
        h, row2(ffn1_norm), w1g_bf, w1u_bf, 0, w1o_bf, row2(final_norm),
        final_norm=False, cast_on_side=later_weights, skip_first_tile=True)
    h = lax.dynamic_update_slice(h, head, (0, 0))
    b_in2 = row2(b_in)
    n, a, b = _mix(h, row2(mix_norm), w_in_bf, b_in2, row2(sgu_ln_g), row2(sgu_ln_b), sgu_w_s,
                   sgu_b_s[:, :, None], bf(pool_w), row2(pool_scale), seq=seq)
    h = _merge(h, n, a, b, w_in_bf, b_in2, w_a_bf, w_b_bf, w_o_bf)
    h, = _ffn(h, row2(ffn2_norm), ffn2_w_in_bf, ffn2_w_in_bf, D_FF // COL_TILE, ffn2_w_out_bf,
              row2(final_norm), final_norm=True)
    return h.reshape(bsz, seq, d).astype(x.dtype)
```

```python
import functools

import jax
import jax.numpy as jnp
from jax import lax
from jax.experimental import pallas as pl
from jax.experimental.pallas import tpu as pltpu

D_MODEL = 2048
D_FF = 5632
CHUNK = 64
SGU_BLOCK = 128
SGU_HEADS = 8
SGU_HEAD_DIM = D_MODEL // SGU_HEADS
POOL_WINDOWS = (2, 4, 8, 16)
POOL_GROUP_DIM = D_MODEL // len(POOL_WINDOWS)
POOL_HALO = 16
RMS_EPS = 1e-6
LN_EPS = 1e-5

ROW_TILE = 512
FFN_ROW_TILE = 1024
NORM_ROWS = 128
COL_TILE = 512
HEAD_COL_TILE = 256
V7X_VMEM_LIMIT_BYTES = 62 * 1024 * 1024

F32 = jnp.float32
BF16 = jnp.bfloat16
LANES = 128
BF16_SUBLANES = 16


def _side_cast_specs(shape, first_tile, n_tiles, n_steps):
    rows, cols = shape
    budget = (n_tiles - first_tile) * n_steps
    block = min(r for r in range(BF16_SUBLANES, rows + 1, BF16_SUBLANES)
                if rows % r == 0 and rows // r <= budget)
    index = lambda i, j: (jnp.clip((i - first_tile) * n_steps + j, 0, rows // block - 1), 0)
    return pl.BlockSpec((block, cols), index)


def _norm_tile(x_ref, g_ref, n_ref, rows, copy_ref=None):
    def step(r, carry):
        rs = pl.ds(pl.multiple_of(r * NORM_ROWS, NORM_ROWS), NORM_ROWS)
        x = x_ref[rs, :]
        rstd = lax.rsqrt(jnp.mean(x * x, axis=-1, keepdims=True) + RMS_EPS)
        n_ref[rs, :] = (x * rstd * g_ref[...]).astype(BF16)
        if copy_ref is not None:
            copy_ref[rs, :] = x
        return carry
    lax.fori_loop(0, rows // NORM_ROWS, step, None)


def _gelu(x):
    return 0.5 * x * (1.0 + lax.erf(x * (2.0 ** -0.5)))


def _dot(a, b):
    return jnp.dot(a, b, preferred_element_type=F32)


def _params(semantics):
    return pltpu.CompilerParams(dimension_semantics=semantics,
                                vmem_limit_bytes=V7X_VMEM_LIMIT_BYTES)


def _ffn_kernel(*refs, final_norm, n_side, weights_f32, skip_first_tile):
    n_emit = 3 if weights_f32 else 0
    x_ref, g_ref, wg_ref, wu_ref, wo_ref, fin_ref = refs[:6]
    side_in = refs[6:6 + n_side]
    o_ref = refs[6 + n_side]
    emit_out = refs[7 + n_side:7 + n_side + n_emit]
    side_out = refs[7 + n_side + n_emit:7 + 2 * n_side + n_emit]
    xn_ref = refs[-1]
    i, j = pl.program_id(0), pl.program_id(1)

    for src, dst in zip(side_in, side_out):
        dst[...] = src[...].astype(BF16)

    def chunk(xn):
        wg, wu, wo = wg_ref[...], wu_ref[...], wo_ref[...]
        if weights_f32:
            wg, wu, wo = wg.astype(BF16), wu.astype(BF16), wo.astype(BF16)
            for dst, w in zip(emit_out, (wg, wu, wo)):
                dst[...] = w
        gate = _dot(xn, wg)
        up = _dot(xn, wu)
        act = (0.5 * gate * jax.nn.sigmoid(gate) * up).astype(BF16)
        return _dot(act, wo)

    active = (i > 0) if skip_first_tile else True

    @pl.when((j == 0) & active)
    def _():
        x = x_ref[...]
        xn = (x * lax.rsqrt(jnp.mean(x * x, axis=-1, keepdims=True) + RMS_EPS) * g_ref[...]).astype(BF16)
        xn_ref[...] = xn
        o_ref[...] = x + chunk(xn)

    @pl.when((j > 0) & active)
    def _():
        o_ref[...] += chunk(xn_ref[...])

    if skip_first_tile:
        @pl.when((i == 0) & (j == 0))
        def _():
            o_ref[...] = jnp.zeros_like(o_ref)

    if final_norm:
        @pl.when(j == pl.num_programs(1) - 1)
        def _():
            def step(r, carry):
                rs = pl.ds(pl.multiple_of(r * NORM_ROWS, NORM_ROWS), NORM_ROWS)
                h = o_ref[rs, :]
                o_ref[rs, :] = h * lax.rsqrt(jnp.mean(h * h, axis=-1, keepdims=True) + RMS_EPS) * fin_ref[...]
                return carry
            lax.fori_loop(0, FFN_ROW_TILE // NORM_ROWS, step, None)


def _ffn(x, norm_g, w_gate, w_up, up_block, w_out, fin_g, *, final_norm, cast_on_side=(),
         skip_first_tile=False):
    s = x.shape[0]
    n_ff = D_FF // COL_TILE
    n_tiles = s // FFN_ROW_TILE
    row = lambda i, j: (i, 0)
    const = lambda i, j: (0, 0)
    first = int(skip_first_tile)
    step = (lambda i, j: jnp.where(i == 0, 0, j)) if skip_first_tile else (lambda i, j: j)
    side_specs = [_side_cast_specs(w.shape, first, n_tiles, n_ff) for w in cast_on_side]
    return pl.pallas_call(
        functools.partial(_ffn_kernel, final_norm=final_norm, n_side=len(cast_on_side),
                          weights_f32=False, skip_first_tile=skip_first_tile),
        grid=(n_tiles, n_ff),
        in_specs=[
            pl.BlockSpec((FFN_ROW_TILE, D_MODEL), lambda i, j: (jnp.maximum(i, first), 0)),
            pl.BlockSpec((1, D_MODEL), const),
            pl.BlockSpec((D_MODEL, COL_TILE), lambda i, j: (0, step(i, j))),
            pl.BlockSpec((D_MODEL, COL_TILE), lambda i, j: (0, step(i, j) + up_block)),
            pl.BlockSpec((COL_TILE, D_MODEL), lambda i, j: (step(i, j), 0)),
            pl.BlockSpec((1, D_MODEL), const),
        ] + side_specs,
        out_specs=[pl.BlockSpec((FFN_ROW_TILE, D_MODEL), row)] + side_specs,
        out_shape=[jax.ShapeDtypeStruct((s, D_MODEL), F32)]
                  + [jax.ShapeDtypeStruct(w.shape, BF16) for w in cast_on_side],
        scratch_shapes=[pltpu.VMEM((FFN_ROW_TILE, D_MODEL), BF16)],
        compiler_params=_params(("arbitrary", "arbitrary")),
        name="ffn_final" if final_norm else "ffn",
    )(x, norm_g, w_gate, w_up, w_out, fin_g, *cast_on_side)


def _ffn_head(x, norm_g, w_in, w_out, fin_g):
    n_ff = D_FF // HEAD_COL_TILE
    const = lambda i, j: (0, 0)
    gate = pl.BlockSpec((D_MODEL, HEAD_COL_TILE), lambda i, j: (0, j))
    down = pl.BlockSpec((HEAD_COL_TILE, D_MODEL), lambda i, j: (j, 0))
    tile = pl.BlockSpec((FFN_ROW_TILE, D_MODEL), const)
    return pl.pallas_call(
        functools.partial(_ffn_kernel, final_norm=False, n_side=0, weights_f32=True,
                          skip_first_tile=False),
        grid=(1, n_ff),
        in_specs=[
            pl.BlockSpec((FFN_ROW_TILE, D_MODEL), const, pipeline_mode=pl.Buffered(1)),
            pl.BlockSpec((1, D_MODEL), const),
            gate,
            pl.BlockSpec((D_MODEL, HEAD_COL_TILE), lambda i, j: (0, j + n_ff)),
            down,
            pl.BlockSpec((1, D_MODEL), const),
        ],
        out_specs=[tile, gate, gate, down],
        out_shape=[jax.ShapeDtypeStruct((FFN_ROW_TILE, D_MODEL), F32),
                   jax.ShapeDtypeStruct((D_MODEL, D_FF), BF16),
                   jax.ShapeDtypeStruct((D_MODEL, D_FF), BF16),
                   jax.ShapeDtypeStruct((D_FF, D_MODEL), BF16)],
        scratch_shapes=[pltpu.VMEM((FFN_ROW_TILE, D_MODEL), BF16)],
        compiler_params=_params(("arbitrary", "arbitrary")),
        name="ffn_head",
    )(x, norm_g, w_in, w_in, w_out, fin_g)


def _mix_kernel(h_ref, g_ref, w_ref, b_ref, lng_ref, lnb_ref, ws_ref, bs_ref, pw_ref, ps_ref,
                n_ref, a_ref, p_ref, v_ref, z_ref, *, seq):
    i, j = pl.program_id(0), pl.program_id(1)

    @pl.when(j == 0)
    def _():
        h = h_ref[...]
        n = (h * lax.rsqrt(jnp.mean(h * h, axis=-1, keepdims=True) + RMS_EPS) * g_ref[...]).astype(BF16)
        n_ref[...] = n
        v_ref[...] = _gelu(_dot(n, w_ref[...]) + b_ref[...])

    @pl.when(j == 1)
    def _():
        n = n_ref[...]
        v = v_ref[...]
        vc = v - jnp.mean(v, axis=-1, keepdims=True)
        rstd = lax.rsqrt(jnp.mean(jnp.square(vc), axis=-1, keepdims=True) + LN_EPS)
        vln = (vc * rstd * lng_ref[...] + lnb_ref[...]).astype(BF16)
        pi = lax.broadcasted_iota(jnp.int32, (SGU_BLOCK, SGU_BLOCK), 0) // CHUNK
        pj = lax.broadcasted_iota(jnp.int32, (SGU_BLOCK, SGU_BLOCK), 1) // CHUNK
        mask = (pj <= pi).astype(F32)
        for head in range(SGU_HEADS):
            cs = slice(head * SGU_HEAD_DIM, (head + 1) * SGU_HEAD_DIM)
            u = _gelu(_dot(n, w_ref[:, cs]) + b_ref[:, cs])
            w = (ws_ref[head] * mask).astype(BF16)
            bias = bs_ref[head]
            for blk in range(ROW_TILE // SGU_BLOCK):
                rs = slice(blk * SGU_BLOCK, (blk + 1) * SGU_BLOCK)
                sp = _dot(w, vln[rs, cs]) + bias
                a_ref[rs, cs] = (u[rs] * sp).astype(BF16)

    t0 = (i * ROW_TILE) % seq

    @pl.when((j == 2) & (t0 == 0))
    def _():
        z_ref[0:POOL_HALO, :] = jnp.zeros((POOL_HALO, D_MODEL), F32)

    @pl.when(j == 2)
    def _():
        n = n_ref[...]
        t = t0 + lax.broadcasted_iota(jnp.int32, (ROW_TILE, 1), 0)
        for k, win in reversed(list(enumerate(POOL_WINDOWS))):
            cs = slice(k * POOL_GROUP_DIM, (k + 1) * POOL_GROUP_DIM)
            z = _dot(n, w_ref[:, cs]) + b_ref[:, cs]
            z_ref[POOL_HALO:, cs] = z
            acc = z_ref[:, cs]
            d = 1
            while d < win:
                acc = acc + pltpu.roll(acc, d, axis=0)
                d *= 2
            inv_cnt = 1.0 / jnp.minimum(t + 1, win).astype(F32)
            pooled = (acc[POOL_HALO:] * inv_cnt - z).astype(BF16)
            mixed = _dot(pooled, pw_ref[k])
            p_ref[:, cs] = (mixed * ps_ref[:, cs]).astype(BF16)
        z_ref[0:POOL_HALO, :] = z_ref[ROW_TILE:ROW_TILE + POOL_HALO, :]


def _mix(h, norm_g, w_in, b_in, ln_g, ln_b, w_s, b_s, pool_w, pool_scale, *, seq):
    s = h.shape[0]
    row = lambda i, j: (i, 0)
    const = lambda i, j: (0, 0)
    const3 = lambda i, j: (0, 0, 0)
    wcol = lambda i, j: (0, jnp.where(j < 2, 1 - j, j))
    act = pl.BlockSpec((ROW_TILE, D_MODEL), row)
    return pl.pallas_call(
        functools.partial(_mix_kernel, seq=seq),
        grid=(s // ROW_TILE, 3),
        in_specs=[
            act,
            pl.BlockSpec((1, D_MODEL), const),
            pl.BlockSpec((D_MODEL, D_MODEL), wcol),
            pl.BlockSpec((1, D_MODEL), wcol),
            pl.BlockSpec((1, D_MODEL), const),
            pl.BlockSpec((1, D_MODEL), const),
            pl.BlockSpec((SGU_HEADS, SGU_BLOCK, SGU_BLOCK), const3),
            pl.BlockSpec((SGU_HEADS, SGU_BLOCK, 1), const3),
            pl.BlockSpec((len(POOL_WINDOWS), POOL_GROUP_DIM, POOL_GROUP_DIM), const3),
            pl.BlockSpec((1, D_MODEL), const),
        ],
        out_specs=[act, act, act],
        out_shape=[jax.ShapeDtypeStruct((s, D_MODEL), BF16)] * 3,
        scratch_shapes=[pltpu.VMEM((ROW_TILE, D_MODEL), F32),
                        pltpu.VMEM((ROW_TILE + POOL_HALO, D_MODEL), F32)],
        compiler_params=_params(("arbitrary", "arbitrary")),
        name="mix",
    )(h, norm_g, w_in, b_in, ln_g, ln_b, w_s, b_s, pool_w, pool_scale)


def _merge_kernel(h_ref, n_ref, a_ref, b_ref, wga_ref, wgb_ref, bga_ref, bgb_ref,
                  wa_ref, wb_ref, wo_ref, o_ref):
    j = pl.program_id(1)

    @pl.when(j == 0)
    def _():
        o_ref[...] = jnp.zeros_like(o_ref)

    col = pl.multiple_of(j * COL_TILE, COL_TILE)
    n = n_ref[...]
    gate_a = jax.nn.sigmoid(_dot(n, wga_ref[...]) + bga_ref[...])
    gate_b = jax.nn.sigmoid(_dot(n, wgb_ref[...]) + bgb_ref[...])
    y_a = _dot(a_ref[...], wa_ref[...])
    y_b = _dot(b_ref[...], wb_ref[...])
    merged = (gate_a * y_a + gate_b * y_b).astype(BF16)
    o_ref[...] += _dot(merged, wo_ref[pl.ds(col, COL_TILE), :])
    o_ref[:, pl.ds(col, COL_TILE)] += h_ref[...]


def _merge(h, n, a, b, w_in, b_in, w_a, w_b, w_o):
    s = h.shape[0]
    n_c = D_MODEL // COL_TILE
    ga0 = 3 * D_MODEL // COL_TILE
    gb0 = 4 * D_MODEL // COL_TILE
    row = lambda i, j: (i, 0)
    act = pl.BlockSpec((ROW_TILE, D_MODEL), row)
    colw = pl.BlockSpec((D_MODEL, COL_TILE), lambda i, j: (0, j))
    return pl.pallas_call(
        _merge_kernel,
        grid=(s // ROW_TILE, n_c),
        in_specs=[
            pl.BlockSpec((ROW_TILE, COL_TILE), lambda i, j: (i, j)),
            act, act, act,
            pl.BlockSpec((D_MODEL, COL_TILE), lambda i, j: (0, ga0 + j)),
            pl.BlockSpec((D_MODEL, COL_TILE), lambda i, j: (0, gb0 + j)),
            pl.BlockSpec((1, COL_TILE), lambda i, j: (0, ga0 + j)),
            pl.BlockSpec((1, COL_TILE), lambda i, j: (0, gb0 + j)),
            colw,
            colw,
            pl.BlockSpec((D_MODEL, D_MODEL), lambda i, j: (0, 0), pipeline_mode=pl.Buffered(1)),
        ],
        out_specs=act,
        out_shape=jax.ShapeDtypeStruct((s, D_MODEL), F32),
        compiler_params=_params(("arbitrary", "arbitrary")),
        name="merge",
    )(h, n, a, b, w_in, w_in, b_in, b_in, w_a, w_b, w_o)


def kernel(x, ffn1_norm, ffn1_w_in, ffn1_w_out, mix_norm, w_in, b_in, sgu_ln_g, sgu_ln_b, sgu_w_s, sgu_b_s, pool_w, pool_scale, w_branch_a, w_branch_b, w_out, ffn2_norm, ffn2_w_in, ffn2_w_out, final_norm):
    bsz, seq, d = x.shape
    assert d == D_MODEL and seq % ROW_TILE == 0 and (bsz * seq) % FFN_ROW_TILE == 0
    row2 = lambda v: v.reshape(1, -1)
    bf = lambda w: w.astype(BF16)

    h = x.reshape(bsz * seq, d)
    later_weights = (ffn2_w_in, ffn2_w_out, w_in, w_branch_a, w_branch_b, w_out)
    head, w1g_bf, w1u_bf, w1o_bf = _ffn_head(h, row2(ffn1_norm), ffn1_w_in, ffn1_w_out, row2(final_norm))
    h, ffn2_w_in_bf, ffn2_w_out_bf, w_in_bf, w_a_bf, w_b_bf, w_o_bf = _ffn(
        h, row2(ffn1_norm), w1g_bf, w1u_bf, 0, w1o_bf, row2(final_norm),
        final_norm=False, cast_on_side=later_weights, skip_first_tile=True)
    h = lax.dynamic_update_slice(h, head, (0, 0))
    b_in2 = row2(b_in)
    n, a, b = _mix(h, row2(mix_norm), w_in_bf, b_in2, row2(sgu_ln_g), row2(sgu_ln_b), sgu_w_s,
                   sgu_b_s[:, :, None], bf(pool_w), row2(pool_scale), seq=seq)
    h = _merge(h, n, a, b, w_in_bf, b_in2, w_a_bf, w_b_bf, w_o_bf)
    h, = _ffn(h, row2(ffn2_norm), ffn2_w_in_bf, ffn2_w_in_bf, D_FF // COL_TILE, ffn2_w_out_bf,
              row2(final_norm), final_norm=True)
    return h.reshape(bsz, seq, d).astype(x.dtype)
```

```python
import functools

import jax
import jax.numpy as jnp
from jax import lax
from jax.experimental import pallas as pl
from jax.experimental.pallas import tpu as pltpu

D_MODEL = 2048
D_FF = 5632
CHUNK = 64
SGU_BLOCK = 128
SGU_HEADS = 8
SGU_HEAD_DIM = D_MODEL // SGU_HEADS
POOL_WINDOWS = (2, 4, 8, 16)
POOL_GROUP_DIM = D_MODEL // len(POOL_WINDOWS)
POOL_HALO = 16
RMS_EPS = 1e-6
LN_EPS = 1e-5

ROW_TILE = 512
FFN_ROW_TILE = 1024
NORM_ROWS = 128
COL_TILE = 512
HEAD_COL_TILE = 256
V7X_VMEM_LIMIT_BYTES = 62 * 1024 * 1024

F32 = jnp.float32
BF16 = jnp.bfloat16
LANES = 128
BF16_SUBLANES = 16


def _side_cast_specs(shape, first_tile, n_tiles, n_steps):
    rows, cols = shape
    budget = (n_tiles - first_tile) * n_steps
    block = min(r for r in range(BF16_SUBLANES, rows + 1, BF16_SUBLANES)
                if rows % r == 0 and rows // r <= budget)
    index = lambda i, j: (jnp.clip((i - first_tile) * n_steps + j, 0, rows // block - 1), 0)
    return pl.BlockSpec((block, cols), index)


def _norm_tile(x_ref, g_ref, n_ref, rows, copy_ref=None):
    def step(r, carry):
        rs = pl.ds(pl.multiple_of(r * NORM_ROWS, NORM_ROWS), NORM_ROWS)
        x = x_ref[rs, :]
        rstd = lax.rsqrt(jnp.mean(x * x, axis=-1, keepdims=True) + RMS_EPS)
        n_ref[rs, :] = (x * rstd * g_ref[...]).astype(BF16)
        if copy_ref is not None:
            copy_ref[rs, :] = x
        return carry
    lax.fori_loop(0, rows // NORM_ROWS, step, None)


def _gelu(x):
    return 0.5 * x * (1.0 + lax.erf(x * (2.0 ** -0.5)))


def _dot(a, b):
    return jnp.dot(a, b, preferred_element_type=F32)


def _params(semantics):
    return pltpu.CompilerParams(dimension_semantics=semantics,
                                vmem_limit_bytes=V7X_VMEM_LIMIT_BYTES)


def _ffn_kernel(*refs, final_norm, n_side, weights_f32, skip_first_tile):
    n_emit = 3 if weights_f32 else 0
    x_ref, g_ref, wg_ref, wu_ref, wo_ref, fin_ref = refs[:6]
    side_in = refs[6:6 + n_side]
    o_ref = refs[6 + n_side]
    emit_out = refs[7 + n_side:7 + n_side + n_emit]
    side_out = refs[7 + n_side + n_emit:7 + 2 * n_side + n_emit]
    xn_ref = refs[-1]
    i, j = pl.program_id(0), pl.program_id(1)

    for src, dst in zip(side_in, side_out):
        dst[...] = src[...].astype(BF16)

    def chunk(xn):
        wg, wu, wo = wg_ref[...], wu_ref[...], wo_ref[...]
        if weights_f32:
            wg, wu, wo = wg.astype(BF16), wu.astype(BF16), wo.astype(BF16)
            for dst, w in zip(emit_out, (wg, wu, wo)):
                dst[...] = w
        gate = _dot(xn, wg)
        up = _dot(xn, wu)
        act = (0.5 * gate * jax.nn.sigmoid(gate) * up).astype(BF16)
        return _dot(act, wo)

    active = (i > 0) if skip_first_tile else True

    @pl.when((j == 0) & active)
    def _():
        x = x_ref[...]
        xn = (x * lax.rsqrt(jnp.mean(x * x, axis=-1, keepdims=True) + RMS_EPS) * g_ref[...]).astype(BF16)
        xn_ref[...] = xn
        o_ref[...] = x + chunk(xn)

    @pl.when((j > 0) & active)
    def _():
        o_ref[...] += chunk(xn_ref[...])

    if skip_first_tile:
        @pl.when((i == 0) & (j == 0))
        def _():
            o_ref[...] = jnp.zeros_like(o_ref)

    if final_norm:
        @pl.when(j == pl.num_programs(1) - 1)
        def _():
            def step(r, carry):
                rs = pl.ds(pl.multiple_of(r * NORM_ROWS, NORM_ROWS), NORM_ROWS)
                h = o_ref[rs, :]
                o_ref[rs, :] = h * lax.rsqrt(jnp.mean(h * h, axis=-1, keepdims=True) + RMS_EPS) * fin_ref[...]
                return carry
            lax.fori_loop(0, FFN_ROW_TILE // NORM_ROWS, step, None)


def _ffn(x, norm_g, w_gate, w_up, up_block, w_out, fin_g, *, final_norm, cast_on_side=(),
         skip_first_tile=False):
    s = x.shape[0]
    n_ff = D_FF // COL_TILE
    n_tiles = s // FFN_ROW_TILE
    row = lambda i, j: (i, 0)
    const = lambda i, j: (0, 0)
    first = int(skip_first_tile)
    step = (lambda i, j: jnp.where(i == 0, 0, j)) if skip_first_tile else (lambda i, j: j)
    side_specs = [_side_cast_specs(w.shape, first, n_tiles, n_ff) for w in cast_on_side]
    return pl.pallas_call(
        functools.partial(_ffn_kernel, final_norm=final_norm, n_side=len(cast_on_side),
                          weights_f32=False, skip_first_tile=skip_first_tile),
        grid=(n_tiles, n_ff),
        in_specs=[
            pl.BlockSpec((FFN_ROW_TILE, D_MODEL), lambda i, j: (jnp.maximum(i, first), 0)),
            pl.BlockSpec((1, D_MODEL), const),
            pl.BlockSpec((D_MODEL, COL_TILE), lambda i, j: (0, step(i, j))),
            pl.BlockSpec((D_MODEL, COL_TILE), lambda i, j: (0, step(i, j) + up_block)),
            pl.BlockSpec((COL_TILE, D_MODEL), lambda i, j: (step(i, j), 0)),
            pl.BlockSpec((1, D_MODEL), const),
        ] + side_specs,
        out_specs=[pl.BlockSpec((FFN_ROW_TILE, D_MODEL), row)] + side_specs,
        out_shape=[jax.ShapeDtypeStruct((s, D_MODEL), F32)]
                  + [jax.ShapeDtypeStruct(w.shape, BF16) for w in cast_on_side],
        scratch_shapes=[pltpu.VMEM((FFN_ROW_TILE, D_MODEL), BF16)],
        compiler_params=_params(("arbitrary", "arbitrary")),
        name="ffn_final" if final_norm else "ffn",
    )(x, norm_g, w_gate, w_up, w_out, fin_g, *cast_on_side)


def _ffn_head(x, norm_g, w_in, w_out, fin_g):
    n_ff = D_FF // HEAD_COL_TILE
    const = lambda i, j: (0, 0)
    gate = pl.BlockSpec((D_MODEL, HEAD_COL_TILE), lambda i, j: (0, j))
    down = pl.BlockSpec((HEAD_COL_TILE, D_MODEL), lambda i, j: (j, 0))
    tile = pl.BlockSpec((FFN_ROW_TILE, D_MODEL), const)
    return pl.pallas_call(
        functools.partial(_ffn_kernel, final_norm=False, n_side=0, weights_f32=True,
                          skip_first_tile=False),
        grid=(1, n_ff),
        in_specs=[
            pl.BlockSpec((FFN_ROW_TILE, D_MODEL), const, pipeline_mode=pl.Buffered(1)),
            pl.BlockSpec((1, D_MODEL), const),
            gate,
            pl.BlockSpec((D_MODEL, HEAD_COL_TILE), lambda i, j: (0, j + n_ff)),
            down,
            pl.BlockSpec((1, D_MODEL), const),
        ],
        out_specs=[tile, gate, gate, down],
        out_shape=[jax.ShapeDtypeStruct((FFN_ROW_TILE, D_MODEL), F32),
                   jax.ShapeDtypeStruct((D_MODEL, D_FF), BF16),
                   jax.ShapeDtypeStruct((D_MODEL, D_FF), BF16),
                   jax.ShapeDtypeStruct((D_FF, D_MODEL), BF16)],
        scratch_shapes=[pltpu.VMEM((FFN_ROW_TILE, D_MODEL), BF16)],
        compiler_params=_params(("arbitrary", "arbitrary")),
        name="ffn_head",
    )(x, norm_g, w_in, w_in, w_out, fin_g)


def _mix_kernel(h_ref, g_ref, w_ref, b_ref, lng_ref, lnb_ref, ws_ref, bs_ref, pw_ref, ps_ref,
                n_ref, a_ref, p_ref, v_ref, z_ref, *, seq):
    i, j = pl.program_id(0), pl.program_id(1)

    @pl.when(j == 0)
    def _():
        h = h_ref[...]
        n = (h * lax.rsqrt(jnp.mean(h * h, axis=-1, keepdims=True) + RMS_EPS) * g_ref[...]).astype(BF16)
        n_ref[...] = n
        v_ref[...] = _gelu(_dot(n, w_ref[...]) + b_ref[...])

    @pl.when(j == 1)
    def _():
        n = n_ref[...]
        v = v_ref[...]
        vc = v - jnp.mean(v, axis=-1, keepdims=True)
        rstd = lax.rsqrt(jnp.mean(jnp.square(vc), axis=-1, keepdims=True) + LN_EPS)
        vln = (vc * rstd * lng_ref[...] + lnb_ref[...]).astype(BF16)
        pi = lax.broadcasted_iota(jnp.int32, (SGU_BLOCK, SGU_BLOCK), 0) // CHUNK
        pj = lax.broadcasted_iota(jnp.int32, (SGU_BLOCK, SGU_BLOCK), 1) // CHUNK
        mask = (pj <= pi).astype(F32)
        for head in range(SGU_HEADS):
            cs = slice(head * SGU_HEAD_DIM, (head + 1) * SGU_HEAD_DIM)
            u = _gelu(_dot(n, w_ref[:, cs]) + b_ref[:, cs])
            w = (ws_ref[head] * mask).astype(BF16)
            bias = bs_ref[head]
            for blk in range(ROW_TILE // SGU_BLOCK):
                rs = slice(blk * SGU_BLOCK, (blk + 1) * SGU_BLOCK)
                sp = _dot(w, vln[rs, cs]) + bias
                a_ref[rs, cs] = (u[rs] * sp).astype(BF16)

    t0 = (i * ROW_TILE) % seq

    @pl.when((j == 2) & (t0 == 0))
    def _():
        z_ref[0:POOL_HALO, :] = jnp.zeros((POOL_HALO, D_MODEL), F32)

    @pl.when(j == 2)
    def _():
        n = n_ref[...]
        t = t0 + lax.broadcasted_iota(jnp.int32, (ROW_TILE, 1), 0)
        for k, win in reversed(list(enumerate(POOL_WINDOWS))):
            cs = slice(k * POOL_GROUP_DIM, (k + 1) * POOL_GROUP_DIM)
            z = _dot(n, w_ref[:, cs]) + b_ref[:, cs]
            z_ref[POOL_HALO:, cs] = z
            acc = z_ref[:, cs]
            d = 1
            while d < win:
                acc = acc + pltpu.roll(acc, d, axis=0)
                d *= 2
            inv_cnt = 1.0 / jnp.minimum(t + 1, win).astype(F32)
            pooled = (acc[POOL_HALO:] * inv_cnt - z).astype(BF16)
            mixed = _dot(pooled, pw_ref[k])
            p_ref[:, cs] = (mixed * ps_ref[:, cs]).astype(BF16)
        z_ref[0:POOL_HALO, :] = z_ref[ROW_TILE:ROW_TILE + POOL_HALO, :]


def _mix(h, norm_g, w_in, b_in, ln_g, ln_b, w_s, b_s, pool_w, pool_scale, *, seq):
    s = h.shape[0]
    row = lambda i, j: (i, 0)
    const = lambda i, j: (0, 0)
    const3 = lambda i, j: (0, 0, 0)
    wcol = lambda i, j: (0, jnp.where(j < 2, 1 - j, j))
    act = pl.BlockSpec((ROW_TILE, D_MODEL), row)
    return pl.pallas_call(
        functools.partial(_mix_kernel, seq=seq),
        grid=(s // ROW_TILE, 3),
        in_specs=[
            act,
            pl.BlockSpec((1, D_MODEL), const),
            pl.BlockSpec((D_MODEL, D_MODEL), wcol),
            pl.BlockSpec((1, D_MODEL), wcol),
            pl.BlockSpec((1, D_MODEL), const),
            pl.BlockSpec((1, D_MODEL), const),
            pl.BlockSpec((SGU_HEADS, SGU_BLOCK, SGU_BLOCK), const3),
            pl.BlockSpec((SGU_HEADS, SGU_BLOCK, 1), const3),
            pl.BlockSpec((len(POOL_WINDOWS), POOL_GROUP_DIM, POOL_GROUP_DIM), const3),
            pl.BlockSpec((1, D_MODEL), const),
        ],
        out_specs=[act, act, act],
        out_shape=[jax.ShapeDtypeStruct((s, D_MODEL), BF16)] * 3,
        scratch_shapes=[pltpu.VMEM((ROW_TILE, D_MODEL), F32),
                        pltpu.VMEM((ROW_TILE + POOL_HALO, D_MODEL), F32)],
        compiler_params=_params(("arbitrary", "arbitrary")),
        name="mix",
    )(h, norm_g, w_in, b_in, ln_g, ln_b, w_s, b_s, pool_w, pool_scale)


def _merge_kernel(h_ref, n_ref, a_ref, b_ref, wga_ref, wgb_ref, bga_ref, bgb_ref,
                  wa_ref, wb_ref, wo_ref, o_ref):
    j = pl.program_id(1)
    col = pl.multiple_of(j * COL_TILE, COL_TILE)

    def chunk():
        n = n_ref[...]
        gate_a = jax.nn.sigmoid(_dot(n, wga_ref[...]) + bga_ref[...])
        gate_b = jax.nn.sigmoid(_dot(n, wgb_ref[...]) + bgb_ref[...])
        y_a = _dot(a_ref[...], wa_ref[:, pl.ds(col, COL_TILE)])
        y_b = _dot(b_ref[...], wb_ref[:, pl.ds(col, COL_TILE)])
        merged = (gate_a * y_a + gate_b * y_b).astype(BF16)
        return _dot(merged, wo_ref[pl.ds(col, COL_TILE), :])

    @pl.when(j == 0)
    def _():
        o_ref[...] = chunk()
        o_ref[:, pl.ds(col, COL_TILE)] += h_ref[...]

    @pl.when(j > 0)
    def _():
        o_ref[...] += chunk()
        o_ref[:, pl.ds(col, COL_TILE)] += h_ref[...]


def _merge(h, n, a, b, w_in, b_in, w_a, w_b, w_o):
    s = h.shape[0]
    n_c = D_MODEL // COL_TILE
    ga0 = 3 * D_MODEL // COL_TILE
    gb0 = 4 * D_MODEL // COL_TILE
    row = lambda i, j: (i, 0)
    act = pl.BlockSpec((ROW_TILE, D_MODEL), row)
    resident = pl.BlockSpec((D_MODEL, D_MODEL), lambda i, j: (0, 0), pipeline_mode=pl.Buffered(1))
    return pl.pallas_call(
        _merge_kernel,
        grid=(s // ROW_TILE, n_c),
        in_specs=[
            pl.BlockSpec((ROW_TILE, COL_TILE), lambda i, j: (i, j)),
            act, act, act,
            pl.BlockSpec((D_MODEL, COL_TILE), lambda i, j: (0, ga0 + j)),
            pl.BlockSpec((D_MODEL, COL_TILE), lambda i, j: (0, gb0 + j)),
            pl.BlockSpec((1, COL_TILE), lambda i, j: (0, ga0 + j)),
            pl.BlockSpec((1, COL_TILE), lambda i, j: (0, gb0 + j)),
            resident, resident, resident,
        ],
        out_specs=act,
        out_shape=jax.ShapeDtypeStruct((s, D_MODEL), F32),
        compiler_params=_params(("arbitrary", "arbitrary")),
        name="merge",
    )(h, n, a, b, w_in, w_in, b_in, b_in, w_a, w_b, w_o)


def kernel(x, ffn1_norm, ffn1_w_in, ffn1_w_out, mix_norm, w_in, b_in, sgu_ln_g, sgu_ln_b, sgu_w_s, sgu_b_s, pool_w, pool_scale, w_branch_a, w_branch_b, w_out, ffn2_norm, ffn2_w_in, ffn2_w_out, final_norm):
    bsz, seq, d = x.shape
    assert d == D_MODEL and seq % ROW_TILE == 0 and (bsz * seq) % FFN_ROW_TILE == 0
    row2 = lambda v: v.reshape(1, -1)
    bf = lambda w: w.astype(BF16)

    h = x.reshape(bsz * seq, d)
    later_weights = (ffn2_w_in, ffn2_w_out, w_in, w_branch_a, w_branch_b, w_out)
    head, w1g_bf, w1u_bf, w1o_bf = _ffn_head(h, row2(ffn1_norm), ffn1_w_in, ffn1_w_out, row2(final_norm))
    h, ffn2_w_in_bf, ffn2_w_out_bf, w_in_bf, w_a_bf, w_b_bf, w_o_bf = _ffn(
        h, row2(ffn1_norm), w1g_bf, w1u_bf, 0, w1o_bf, row2(final_norm),
        final_norm=False, cast_on_side=later_weights, skip_first_tile=True)
    h = lax.dynamic_update_slice(h, head, (0, 0))
    b_in2 = row2(b_in)
    n, a, b = _mix(h, row2(mix_norm), w_in_bf, b_in2, row2(sgu_ln_g), row2(sgu_ln_b), sgu_w_s,
                   sgu_b_s[:, :, None], bf(pool_w), row2(pool_scale), seq=seq)
    h = _merge(h, n, a, b, w_in_bf, b_in2, w_a_bf, w_b_bf, w_o_bf)
    h, = _ffn(h, row2(ffn2_norm), ffn2_w_in_bf, ffn2_w_in_bf, D_FF // COL_TILE, ffn2_w_out_bf,
              row2(final_norm), final_norm=True)
    return h.reshape(bsz, seq, d).astype(x.dtype)
```

```python
import functools

import jax
import jax.numpy as jnp
from jax import lax
from jax.experimental import pallas as pl
from jax.experimental.pallas import tpu as pltpu

D_MODEL = 2048
D_FF = 5632
CHUNK = 64
SGU_BLOCK = 128
SGU_HEADS = 8
SGU_HEAD_DIM = D_MODEL // SGU_HEADS
POOL_WINDOWS = (2, 4, 8, 16)
POOL_GROUP_DIM = D_MODEL // len(POOL_WINDOWS)
POOL_HALO = 16
RMS_EPS = 1e-6
LN_EPS = 1e-5

ROW_TILE = 512
FFN_ROW_TILE = 1024
NORM_ROWS = 128
COL_TILE = 512
HEAD_COL_TILE = 256
V7X_VMEM_LIMIT_BYTES = 62 * 1024 * 1024

F32 = jnp.float32
BF16 = jnp.bfloat16
LANES = 128
BF16_SUBLANES = 16


def _side_cast_specs(shape, first_tile, n_tiles, n_steps):
    rows, cols = shape
    budget = (n_tiles - first_tile) * n_steps
    block = min(r for r in range(BF16_SUBLANES, rows + 1, BF16_SUBLANES)
                if rows % r == 0 and rows // r <= budget)
    index = lambda i, j: (jnp.clip((i - first_tile) * n_steps + j, 0, rows // block - 1), 0)
    return pl.BlockSpec((block, cols), index)


def _rms_bf16(x, g):
    return (x * lax.rsqrt(jnp.mean(x * x, axis=-1, keepdims=True) + RMS_EPS) * g).astype(BF16)


def _gelu(x):
    return 0.5 * x * (1.0 + lax.erf(x * (2.0 ** -0.5)))


def _dot(a, b):
    return jnp.dot(a, b, preferred_element_type=F32)


def _params(semantics):
    return pltpu.CompilerParams(dimension_semantics=semantics,
                                vmem_limit_bytes=V7X_VMEM_LIMIT_BYTES)


def _ffn_kernel(*refs, final_norm, n_side, weights_f32, skip_first_tile):
    n_emit = 3 if weights_f32 else 0
    x_ref, g_ref, wg_ref, wu_ref, wo_ref, fin_ref = refs[:6]
    side_in = refs[6:6 + n_side]
    o_ref = refs[6 + n_side]
    emit_out = refs[7 + n_side:7 + n_side + n_emit]
    side_out = refs[7 + n_side + n_emit:7 + 2 * n_side + n_emit]
    xn_ref = refs[-1]
    i, j = pl.program_id(0), pl.program_id(1)

    for src, dst in zip(side_in, side_out):
        dst[...] = src[...].astype(BF16)

    def chunk(xn):
        wg, wu, wo = wg_ref[...], wu_ref[...], wo_ref[...]
        if weights_f32:
            wg, wu, wo = wg.astype(BF16), wu.astype(BF16), wo.astype(BF16)
            for dst, w in zip(emit_out, (wg, wu, wo)):
                dst[...] = w
        gate = _dot(xn, wg)
        up = _dot(xn, wu)
        act = (0.5 * gate * jax.nn.sigmoid(gate) * up).astype(BF16)
        return _dot(act, wo)

    active = (i > 0) if skip_first_tile else True

    @pl.when((j == 0) & active)
    def _():
        x = x_ref[...]
        xn = _rms_bf16(x, g_ref[...])
        xn_ref[...] = xn
        o_ref[...] = x + chunk(xn)

    @pl.when((j > 0) & active)
    def _():
        o_ref[...] += chunk(xn_ref[...])

    if skip_first_tile:
        @pl.when((i == 0) & (j == 0))
        def _():
            o_ref[...] = jnp.zeros_like(o_ref)

    if final_norm:
        @pl.when(j == pl.num_programs(1) - 1)
        def _():
            def step(r, carry):
                rs = pl.ds(pl.multiple_of(r * NORM_ROWS, NORM_ROWS), NORM_ROWS)
                h = o_ref[rs, :]
                o_ref[rs, :] = h * lax.rsqrt(jnp.mean(h * h, axis=-1, keepdims=True) + RMS_EPS) * fin_ref[...]
                return carry
            lax.fori_loop(0, FFN_ROW_TILE // NORM_ROWS, step, None)


def _ffn(x, norm_g, w_gate, w_up, up_block, w_out, fin_g, *, final_norm, cast_on_side=(),
         skip_first_tile=False):
    s = x.shape[0]
    n_ff = D_FF // COL_TILE
    n_tiles = s // FFN_ROW_TILE
    row = lambda i, j: (i, 0)
    const = lambda i, j: (0, 0)
    first = int(skip_first_tile)
    step = (lambda i, j: jnp.where(i == 0, 0, j)) if skip_first_tile else (lambda i, j: j)
    side_specs = [_side_cast_specs(w.shape, first, n_tiles, n_ff) for w in cast_on_side]
    return pl.pallas_call(
        functools.partial(_ffn_kernel, final_norm=final_norm, n_side=len(cast_on_side),
                          weights_f32=False, skip_first_tile=skip_first_tile),
        grid=(n_tiles, n_ff),
        in_specs=[
            pl.BlockSpec((FFN_ROW_TILE, D_MODEL), lambda i, j: (jnp.maximum(i, first), 0)),
            pl.BlockSpec((1, D_MODEL), const),
            pl.BlockSpec((D_MODEL, COL_TILE), lambda i, j: (0, step(i, j))),
            pl.BlockSpec((D_MODEL, COL_TILE), lambda i, j: (0, step(i, j) + up_block)),
            pl.BlockSpec((COL_TILE, D_MODEL), lambda i, j: (step(i, j), 0)),
            pl.BlockSpec((1, D_MODEL), const),
        ] + side_specs,
        out_specs=[pl.BlockSpec((FFN_ROW_TILE, D_MODEL), row)] + side_specs,
        out_shape=[jax.ShapeDtypeStruct((s, D_MODEL), F32)]
                  + [jax.ShapeDtypeStruct(w.shape, BF16) for w in cast_on_side],
        scratch_shapes=[pltpu.VMEM((FFN_ROW_TILE, D_MODEL), BF16)],
        compiler_params=_params(("arbitrary", "arbitrary")),
        name="ffn_final" if final_norm else "ffn",
    )(x, norm_g, w_gate, w_up, w_out, fin_g, *cast_on_side)


def _ffn_head(x, norm_g, w_in, w_out, fin_g):
    n_ff = D_FF // HEAD_COL_TILE
    const = lambda i, j: (0, 0)
    gate = pl.BlockSpec((D_MODEL, HEAD_COL_TILE), lambda i, j: (0, j))
    down = pl.BlockSpec((HEAD_COL_TILE, D_MODEL), lambda i, j: (j, 0))
    tile = pl.BlockSpec((FFN_ROW_TILE, D_MODEL), const)
    return pl.pallas_call(
        functools.partial(_ffn_kernel, final_norm=False, n_side=0, weights_f32=True,
                          skip_first_tile=False),
        grid=(1, n_ff),
        in_specs=[
            pl.BlockSpec((FFN_ROW_TILE, D_MODEL), const, pipeline_mode=pl.Buffered(1)),
            pl.BlockSpec((1, D_MODEL), const),
            gate,
            pl.BlockSpec((D_MODEL, HEAD_COL_TILE), lambda i, j: (0, j + n_ff)),
            down,
            pl.BlockSpec((1, D_MODEL), const),
        ],
        out_specs=[tile, gate, gate, down],
        out_shape=[jax.ShapeDtypeStruct((FFN_ROW_TILE, D_MODEL), F32),
                   jax.ShapeDtypeStruct((D_MODEL, D_FF), BF16),
                   jax.ShapeDtypeStruct((D_MODEL, D_FF), BF16),
                   jax.ShapeDtypeStruct((D_FF, D_MODEL), BF16)],
        scratch_shapes=[pltpu.VMEM((FFN_ROW_TILE, D_MODEL), BF16)],
        compiler_params=_params(("arbitrary", "arbitrary")),
        name="ffn_head",
    )(x, norm_g, w_in, w_in, w_out, fin_g)


def _mix_kernel(h_ref, g_ref, w_ref, b_ref, lng_ref, lnb_ref, ws_ref, bs_ref, pw_ref, ps_ref,
                n_ref, a_ref, p_ref, v_ref, z_ref, *, seq):
    i, j = pl.program_id(0), pl.program_id(1)

    @pl.when(j == 0)
    def _():
        n = _rms_bf16(h_ref[...], g_ref[...])
        n_ref[...] = n
        v_ref[...] = _gelu(_dot(n, w_ref[...]) + b_ref[...])

    @pl.when(j == 1)
    def _():
        n = n_ref[...]
        v = v_ref[...]
        vc = v - jnp.mean(v, axis=-1, keepdims=True)
        rstd = lax.rsqrt(jnp.mean(jnp.square(vc), axis=-1, keepdims=True) + LN_EPS)
        vln = (vc * rstd * lng_ref[...] + lnb_ref[...]).astype(BF16)
        pi = lax.broadcasted_iota(jnp.int32, (SGU_BLOCK, SGU_BLOCK), 0) // CHUNK
        pj = lax.broadcasted_iota(jnp.int32, (SGU_BLOCK, SGU_BLOCK), 1) // CHUNK
        mask = (pj <= pi).astype(F32)
        for head in range(SGU_HEADS):
            cs = slice(head * SGU_HEAD_DIM, (head + 1) * SGU_HEAD_DIM)
            u = _gelu(_dot(n, w_ref[:, cs]) + b_ref[:, cs])
            w = (ws_ref[head] * mask).astype(BF16)
            bias = bs_ref[head]
            for blk in range(ROW_TILE // SGU_BLOCK):
                rs = slice(blk * SGU_BLOCK, (blk + 1) * SGU_BLOCK)
                sp = _dot(w, vln[rs, cs]) + bias
                a_ref[rs, cs] = (u[rs] * sp).astype(BF16)

    t0 = (i * ROW_TILE) % seq

    @pl.when((j == 2) & (t0 == 0))
    def _():
        z_ref[0:POOL_HALO, :] = jnp.zeros((POOL_HALO, D_MODEL), F32)

    @pl.when(j == 2)
    def _():
        n = n_ref[...]
        t = t0 + lax.broadcasted_iota(jnp.int32, (ROW_TILE, 1), 0)
        for k, win in reversed(list(enumerate(POOL_WINDOWS))):
            cs = slice(k * POOL_GROUP_DIM, (k + 1) * POOL_GROUP_DIM)
            z = _dot(n, w_ref[:, cs]) + b_ref[:, cs]
            z_ref[POOL_HALO:, cs] = z
            acc = z_ref[:, cs]
            d = 1
            while d < win:
                acc = acc + pltpu.roll(acc, d, axis=0)
                d *= 2
            inv_cnt = 1.0 / jnp.minimum(t + 1, win).astype(F32)
            pooled = (acc[POOL_HALO:] * inv_cnt - z).astype(BF16)
            mixed = _dot(pooled, pw_ref[k])
            p_ref[:, cs] = (mixed * ps_ref[:, cs]).astype(BF16)
        z_ref[0:POOL_HALO, :] = z_ref[ROW_TILE:ROW_TILE + POOL_HALO, :]


def _mix(h, norm_g, w_in, b_in, ln_g, ln_b, w_s, b_s, pool_w, pool_scale, *, seq):
    s = h.shape[0]
    row = lambda i, j: (i, 0)
    const = lambda i, j: (0, 0)
    const3 = lambda i, j: (0, 0, 0)
    wcol = lambda i, j: (0, jnp.where(j < 2, 1 - j, j))
    act = pl.BlockSpec((ROW_TILE, D_MODEL), row)
    return pl.pallas_call(
        functools.partial(_mix_kernel, seq=seq),
        grid=(s // ROW_TILE, 3),
        in_specs=[
            act,
            pl.BlockSpec((1, D_MODEL), const),
            pl.BlockSpec((D_MODEL, D_MODEL), wcol),
            pl.BlockSpec((1, D_MODEL), wcol),
            pl.BlockSpec((1, D_MODEL), const),
            pl.BlockSpec((1, D_MODEL), const),
            pl.BlockSpec((SGU_HEADS, SGU_BLOCK, SGU_BLOCK), const3),
            pl.BlockSpec((SGU_HEADS, SGU_BLOCK, 1), const3),
            pl.BlockSpec((len(POOL_WINDOWS), POOL_GROUP_DIM, POOL_GROUP_DIM), const3),
            pl.BlockSpec((1, D_MODEL), const),
        ],
        out_specs=[act, act, act],
        out_shape=[jax.ShapeDtypeStruct((s, D_MODEL), BF16)] * 3,
        scratch_shapes=[pltpu.VMEM((ROW_TILE, D_MODEL), F32),
                        pltpu.VMEM((ROW_TILE + POOL_HALO, D_MODEL), F32)],
        compiler_params=_params(("arbitrary", "arbitrary")),
        name="mix",
    )(h, norm_g, w_in, b_in, ln_g, ln_b, w_s, b_s, pool_w, pool_scale)


def _merge_kernel(h_ref, n_ref, a_ref, b_ref, wga_ref, wgb_ref, bga_ref, bgb_ref,
                  wa_ref, wb_ref, wo_ref, o_ref):
    j = pl.program_id(1)
    col = pl.multiple_of(j * COL_TILE, COL_TILE)

    def chunk():
        n = n_ref[...]
        gate_a = jax.nn.sigmoid(_dot(n, wga_ref[...]) + bga_ref[...])
        gate_b = jax.nn.sigmoid(_dot(n, wgb_ref[...]) + bgb_ref[...])
        y_a = _dot(a_ref[...], wa_ref[:, pl.ds(col, COL_TILE)])
        y_b = _dot(b_ref[...], wb_ref[:, pl.ds(col, COL_TILE)])
        merged = (gate_a * y_a + gate_b * y_b).astype(BF16)
        return _dot(merged, wo_ref[pl.ds(col, COL_TILE), :])

    @pl.when(j == 0)
    def _():
        o_ref[...] = chunk()
        o_ref[:, pl.ds(col, COL_TILE)] += h_ref[...]

    @pl.when(j > 0)
    def _():
        o_ref[...] += chunk()
        o_ref[:, pl.ds(col, COL_TILE)] += h_ref[...]


def _merge(h, n, a, b, w_in, b_in, w_a, w_b, w_o):
    s = h.shape[0]
    n_c = D_MODEL // COL_TILE
    ga0 = 3 * D_MODEL // COL_TILE
    gb0 = 4 * D_MODEL // COL_TILE
    row = lambda i, j: (i, 0)
    act = pl.BlockSpec((ROW_TILE, D_MODEL), row)
    resident = pl.BlockSpec((D_MODEL, D_MODEL), lambda i, j: (0, 0), pipeline_mode=pl.Buffered(1))
    return pl.pallas_call(
        _merge_kernel,
        grid=(s // ROW_TILE, n_c),
        in_specs=[
            pl.BlockSpec((ROW_TILE, COL_TILE), lambda i, j: (i, j)),
            act, act, act,
            pl.BlockSpec((D_MODEL, COL_TILE), lambda i, j: (0, ga0 + j)),
            pl.BlockSpec((D_MODEL, COL_TILE), lambda i, j: (0, gb0 + j)),
            pl.BlockSpec((1, COL_TILE), lambda i, j: (0, ga0 + j)),
            pl.BlockSpec((1, COL_TILE), lambda i, j: (0, gb0 + j)),
            resident, resident, resident,
        ],
        out_specs=act,
        out_shape=jax.ShapeDtypeStruct((s, D_MODEL), F32),
        compiler_params=_params(("arbitrary", "arbitrary")),
        name="merge",
    )(h, n, a, b, w_in, w_in, b_in, b_in, w_a, w_b, w_o)


def kernel(x, ffn1_norm, ffn1_w_in, ffn1_w_out, mix_norm, w_in, b_in, sgu_ln_g, sgu_ln_b, sgu_w_s, sgu_b_s, pool_w, pool_scale, w_branch_a, w_branch_b, w_out, ffn2_norm, ffn2_w_in, ffn2_w_out, final_norm):
    bsz, seq, d = x.shape
    assert d == D_MODEL and seq % ROW_TILE == 0 and (bsz * seq) % FFN_ROW_TILE == 0
    row2 = lambda v: v.reshape(1, -1)
    bf = lambda w: w.astype(BF16)

    h = x.reshape(bsz * seq, d)
    later_weights = (ffn2_w_in, ffn2_w_out, w_in, w_branch_a, w_branch_b, w_out)
    head, w1g_bf, w1u_bf, w1o_bf = _ffn_head(h, row2(ffn1_norm), ffn1_w_in, ffn1_w_out, row2(final_norm))
    h, ffn2_w_in_bf, ffn2_w_out_bf, w_in_bf, w_a_bf, w_b_bf, w_o_bf = _ffn(
        h, row2(ffn1_norm), w1g_bf, w1u_bf, 0, w1o_bf, row2(final_norm),
        final_norm=False, cast_on_side=later_weights, skip_first_tile=True)
    h = lax.dynamic_update_slice(h, head, (0, 0))
    b_in2 = row2(b_in)
    n, a, b = _mix(h, row2(mix_norm), w_in_bf, b_in2, row2(sgu_ln_g), row2(sgu_ln_b), sgu_w_s,
                   sgu_b_s[:, :, None], bf(pool_w), row2(pool_scale), seq=seq)
    h = _merge(h, n, a, b, w_in_bf, b_in2, w_a_bf, w_b_bf, w_o_bf)
    h, = _ffn(h, row2(ffn2_norm), ffn2_w_in_bf, ffn2_w_in_bf, D_FF // COL_TILE, ffn2_w_out_bf,
              row2(final_norm), final_norm=True)
    return h.reshape(bsz, seq, d).astype(x.dtype)
```

```python
import functools

import jax
import jax.numpy as jnp
from jax import lax
from jax.experimental import pallas as pl
from jax.experimental.pallas import tpu as pltpu

D_MODEL = 2048
D_FF = 5632
CHUNK = 64
SGU_BLOCK = 128
SGU_HEADS = 8
SGU_HEAD_DIM = D_MODEL // SGU_HEADS
POOL_WINDOWS = (2, 4, 8, 16)
POOL_GROUP_DIM = D_MODEL // len(POOL_WINDOWS)
POOL_HALO = 16
RMS_EPS = 1e-6
LN_EPS = 1e-5

ROW_TILE = 512
FFN_ROW_TILE = 1024
COL_TILE = 512
HEAD_COL_TILE = 256
V7X_VMEM_LIMIT_BYTES = 62 * 1024 * 1024

F32 = jnp.float32
BF16 = jnp.bfloat16
LANES = 128
BF16_SUBLANES = 16


def _side_cast_specs(shape, first_tile, n_tiles, n_steps):
    rows, cols = shape
    budget = (n_tiles - first_tile) * n_steps
    block = min(r for r in range(BF16_SUBLANES, rows + 1, BF16_SUBLANES)
                if rows % r == 0 and rows // r <= budget)
    index = lambda i, j: (jnp.clip((i - first_tile) * n_steps + j, 0, rows // block - 1), 0)
    return pl.BlockSpec((block, cols), index)


def _rms_bf16(x, g):
    return (x * lax.rsqrt(jnp.mean(x * x, axis=-1, keepdims=True) + RMS_EPS) * g).astype(BF16)


def _gelu(x):
    return 0.5 * x * (1.0 + lax.erf(x * (2.0 ** -0.5)))


def _dot(a, b):
    return jnp.dot(a, b, preferred_element_type=F32)


def _params(semantics):
    return pltpu.CompilerParams(dimension_semantics=semantics,
                                vmem_limit_bytes=V7X_VMEM_LIMIT_BYTES)


def _ffn_kernel(*refs, final_norm, n_side, weights_f32, skip_first_tile):
    n_emit = 3 if weights_f32 else 0
    x_ref, g_ref, wg_ref, wu_ref, wo_ref, fin_ref = refs[:6]
    side_in = refs[6:6 + n_side]
    o_ref = refs[6 + n_side]
    emit_out = refs[7 + n_side:7 + n_side + n_emit]
    side_out = refs[7 + n_side + n_emit:7 + 2 * n_side + n_emit]
    xn_ref = refs[-1]
    i, j = pl.program_id(0), pl.program_id(1)

    for src, dst in zip(side_in, side_out):
        dst[...] = src[...].astype(BF16)

    def chunk(xn):
        wg, wu, wo = wg_ref[...], wu_ref[...], wo_ref[...]
        if weights_f32:
            wg, wu, wo = wg.astype(BF16), wu.astype(BF16), wo.astype(BF16)
            for dst, w in zip(emit_out, (wg, wu, wo)):
                dst[...] = w
        gate = _dot(xn, wg)
        up = _dot(xn, wu)
        act = (0.5 * gate * jax.nn.sigmoid(gate) * up).astype(BF16)
        return _dot(act, wo)

    active = (i > 0) if skip_first_tile else True

    @pl.when((j == 0) & active)
    def _():
        x = x_ref[...]
        xn = _rms_bf16(x, g_ref[...])
        xn_ref[...] = xn
        o_ref[...] = x + chunk(xn)

    last = pl.num_programs(1) - 1
    middle = (j > 0) & ((j < last) if final_norm else True)

    @pl.when(middle & active)
    def _():
        o_ref[...] += chunk(xn_ref[...])

    if skip_first_tile:
        @pl.when((i == 0) & (j == 0))
        def _():
            o_ref[...] = jnp.zeros_like(o_ref)

    if final_norm:
        @pl.when(j == last)
        def _():
            half = FFN_ROW_TILE // 2
            for rs in (slice(0, half), slice(half, FFN_ROW_TILE)):
                h = o_ref[rs, :] + chunk(xn_ref[rs, :])
                o_ref[rs, :] = h * lax.rsqrt(jnp.mean(h * h, axis=-1, keepdims=True) + RMS_EPS) * fin_ref[...]


def _ffn(x, norm_g, w_gate, w_up, up_block, w_out, fin_g, *, final_norm, cast_on_side=(),
         skip_first_tile=False):
    s = x.shape[0]
    n_ff = D_FF // COL_TILE
    n_tiles = s // FFN_ROW_TILE
    row = lambda i, j: (i, 0)
    const = lambda i, j: (0, 0)
    first = int(skip_first_tile)
    step = (lambda i, j: jnp.where(i == 0, 0, j)) if skip_first_tile else (lambda i, j: j)
    side_specs = [_side_cast_specs(w.shape, first, n_tiles, n_ff) for w in cast_on_side]
    return pl.pallas_call(
        functools.partial(_ffn_kernel, final_norm=final_norm, n_side=len(cast_on_side),
                          weights_f32=False, skip_first_tile=skip_first_tile),
        grid=(n_tiles, n_ff),
        in_specs=[
            pl.BlockSpec((FFN_ROW_TILE, D_MODEL), lambda i, j: (jnp.maximum(i, first), 0)),
            pl.BlockSpec((1, D_MODEL), const),
            pl.BlockSpec((D_MODEL, COL_TILE), lambda i, j: (0, step(i, j))),
            pl.BlockSpec((D_MODEL, COL_TILE), lambda i, j: (0, step(i, j) + up_block)),
            pl.BlockSpec((COL_TILE, D_MODEL), lambda i, j: (step(i, j), 0)),
            pl.BlockSpec((1, D_MODEL), const),
        ] + side_specs,
        out_specs=[pl.BlockSpec((FFN_ROW_TILE, D_MODEL), row)] + side_specs,
        out_shape=[jax.ShapeDtypeStruct((s, D_MODEL), F32)]
                  + [jax.ShapeDtypeStruct(w.shape, BF16) for w in cast_on_side],
        scratch_shapes=[pltpu.VMEM((FFN_ROW_TILE, D_MODEL), BF16)],
        compiler_params=_params(("arbitrary", "arbitrary")),
        name="ffn_final" if final_norm else "ffn",
    )(x, norm_g, w_gate, w_up, w_out, fin_g, *cast_on_side)


def _ffn_head(x, norm_g, w_in, w_out, fin_g):
    n_ff = D_FF // HEAD_COL_TILE
    const = lambda i, j: (0, 0)
    gate = pl.BlockSpec((D_MODEL, HEAD_COL_TILE), lambda i, j: (0, j))
    down = pl.BlockSpec((HEAD_COL_TILE, D_MODEL), lambda i, j: (j, 0))
    tile = pl.BlockSpec((FFN_ROW_TILE, D_MODEL), const)
    return pl.pallas_call(
        functools.partial(_ffn_kernel, final_norm=False, n_side=0, weights_f32=True,
                          skip_first_tile=False),
        grid=(1, n_ff),
        in_specs=[
            pl.BlockSpec((FFN_ROW_TILE, D_MODEL), const, pipeline_mode=pl.Buffered(1)),
            pl.BlockSpec((1, D_MODEL), const),
            gate,
            pl.BlockSpec((D_MODEL, HEAD_COL_TILE), lambda i, j: (0, j + n_ff)),
            down,
            pl.BlockSpec((1, D_MODEL), const),
        ],
        out_specs=[tile, gate, gate, down],
        out_shape=[jax.ShapeDtypeStruct((FFN_ROW_TILE, D_MODEL), F32),
                   jax.ShapeDtypeStruct((D_MODEL, D_FF), BF16),
                   jax.ShapeDtypeStruct((D_MODEL, D_FF), BF16),
                   jax.ShapeDtypeStruct((D_FF, D_MODEL), BF16)],
        scratch_shapes=[pltpu.VMEM((FFN_ROW_TILE, D_MODEL), BF16)],
        compiler_params=_params(("arbitrary", "arbitrary")),
        name="ffn_head",
    )(x, norm_g, w_in, w_in, w_out, fin_g)


def _mix_kernel(h_ref, g_ref, w_ref, b_ref, lng_ref, lnb_ref, ws_ref, bs_ref, pw_ref, ps_ref,
                n_ref, a_ref, p_ref, v_ref, z_ref, *, seq):
    i, j = pl.program_id(0), pl.program_id(1)

    @pl.when(j == 0)
    def _():
        n = _rms_bf16(h_ref[...], g_ref[...])
        n_ref[...] = n
        v_ref[...] = _gelu(_dot(n, w_ref[...]) + b_ref[...])

    @pl.when(j == 1)
    def _():
        n = n_ref[...]
        v = v_ref[...]
        vc = v - jnp.mean(v, axis=-1, keepdims=True)
        rstd = lax.rsqrt(jnp.mean(jnp.square(vc), axis=-1, keepdims=True) + LN_EPS)
        vln = (vc * rstd * lng_ref[...] + lnb_ref[...]).astype(BF16)
        pi = lax.broadcasted_iota(jnp.int32, (SGU_BLOCK, SGU_BLOCK), 0) // CHUNK
        pj = lax.broadcasted_iota(jnp.int32, (SGU_BLOCK, SGU_BLOCK), 1) // CHUNK
        mask = (pj <= pi).astype(F32)
        for head in range(SGU_HEADS):
            cs = slice(head * SGU_HEAD_DIM, (head + 1) * SGU_HEAD_DIM)
            u = _gelu(_dot(n, w_ref[:, cs]) + b_ref[:, cs])
            w = (ws_ref[head] * mask).astype(BF16)
            bias = bs_ref[head]
            for blk in range(ROW_TILE // SGU_BLOCK):
                rs = slice(blk * SGU_BLOCK, (blk + 1) * SGU_BLOCK)
                sp = _dot(w, vln[rs, cs]) + bias
                a_ref[rs, cs] = (u[rs] * sp).astype(BF16)

    t0 = (i * ROW_TILE) % seq

    @pl.when((j == 2) & (t0 == 0))
    def _():
        z_ref[0:POOL_HALO, :] = jnp.zeros((POOL_HALO, D_MODEL), F32)

    @pl.when(j == 2)
    def _():
        n = n_ref[...]
        t = t0 + lax.broadcasted_iota(jnp.int32, (ROW_TILE, 1), 0)
        for k, win in reversed(list(enumerate(POOL_WINDOWS))):
            cs = slice(k * POOL_GROUP_DIM, (k + 1) * POOL_GROUP_DIM)
            z = _dot(n, w_ref[:, cs]) + b_ref[:, cs]
            z_ref[POOL_HALO:, cs] = z
            acc = z_ref[:, cs]
            d = 1
            while d < win:
                acc = acc + pltpu.roll(acc, d, axis=0)
                d *= 2
            inv_cnt = 1.0 / jnp.minimum(t + 1, win).astype(F32)
            pooled = (acc[POOL_HALO:] * inv_cnt - z).astype(BF16)
            mixed = _dot(pooled, pw_ref[k])
            p_ref[:, cs] = (mixed * ps_ref[:, cs]).astype(BF16)
        z_ref[0:POOL_HALO, :] = z_ref[ROW_TILE:ROW_TILE + POOL_HALO, :]


def _mix(h, norm_g, w_in, b_in, ln_g, ln_b, w_s, b_s, pool_w, pool_scale, *, seq):
    s = h.shape[0]
    row = lambda i, j: (i, 0)
    const = lambda i, j: (0, 0)
    const3 = lambda i, j: (0, 0, 0)
    wcol = lambda i, j: (0, jnp.where(j < 2, 1 - j, j))
    act = pl.BlockSpec((ROW_TILE, D_MODEL), row)
    return pl.pallas_call(
        functools.partial(_mix_kernel, seq=seq),
        grid=(s // ROW_TILE, 3),
        in_specs=[
            act,
            pl.BlockSpec((1, D_MODEL), const),
            pl.BlockSpec((D_MODEL, D_MODEL), wcol),
            pl.BlockSpec((1, D_MODEL), wcol),
            pl.BlockSpec((1, D_MODEL), const),
            pl.BlockSpec((1, D_MODEL), const),
            pl.BlockSpec((SGU_HEADS, SGU_BLOCK, SGU_BLOCK), const3),
            pl.BlockSpec((SGU_HEADS, SGU_BLOCK, 1), const3),
            pl.BlockSpec((len(POOL_WINDOWS), POOL_GROUP_DIM, POOL_GROUP_DIM), const3),
            pl.BlockSpec((1, D_MODEL), const),
        ],
        out_specs=[act, act, act],
        out_shape=[jax.ShapeDtypeStruct((s, D_MODEL), BF16)] * 3,
        scratch_shapes=[pltpu.VMEM((ROW_TILE, D_MODEL), F32),
                        pltpu.VMEM((ROW_TILE + POOL_HALO, D_MODEL), F32)],
        compiler_params=_params(("arbitrary", "arbitrary")),
        name="mix",
    )(h, norm_g, w_in, b_in, ln_g, ln_b, w_s, b_s, pool_w, pool_scale)


def _merge_kernel(h_ref, n_ref, a_ref, b_ref, wga_ref, wgb_ref, bga_ref, bgb_ref,
                  wa_ref, wb_ref, wo_ref, o_ref):
    j = pl.program_id(1)
    col = pl.multiple_of(j * COL_TILE, COL_TILE)

    def chunk():
        n = n_ref[...]
        gate_a = jax.nn.sigmoid(_dot(n, wga_ref[...]) + bga_ref[...])
        gate_b = jax.nn.sigmoid(_dot(n, wgb_ref[...]) + bgb_ref[...])
        y_a = _dot(a_ref[...], wa_ref[:, pl.ds(col, COL_TILE)])
        y_b = _dot(b_ref[...], wb_ref[:, pl.ds(col, COL_TILE)])
        merged = (gate_a * y_a + gate_b * y_b).astype(BF16)
        return _dot(merged, wo_ref[pl.ds(col, COL_TILE), :])

    @pl.when(j == 0)
    def _():
        o_ref[...] = chunk()
        o_ref[:, pl.ds(col, COL_TILE)] += h_ref[...]

    @pl.when(j > 0)
    def _():
        o_ref[...] += chunk()
        o_ref[:, pl.ds(col, COL_TILE)] += h_ref[...]


def _merge(h, n, a, b, w_in, b_in, w_a, w_b, w_o):
    s = h.shape[0]
    n_c = D_MODEL // COL_TILE
    ga0 = 3 * D_MODEL // COL_TILE
    gb0 = 4 * D_MODEL // COL_TILE
    row = lambda i, j: (i, 0)
    act = pl.BlockSpec((ROW_TILE, D_MODEL), row)
    resident = pl.BlockSpec((D_MODEL, D_MODEL), lambda i, j: (0, 0), pipeline_mode=pl.Buffered(1))
    return pl.pallas_call(
        _merge_kernel,
        grid=(s // ROW_TILE, n_c),
        in_specs=[
            pl.BlockSpec((ROW_TILE, COL_TILE), lambda i, j: (i, j)),
            act, act, act,
            pl.BlockSpec((D_MODEL, COL_TILE), lambda i, j: (0, ga0 + j)),
            pl.BlockSpec((D_MODEL, COL_TILE), lambda i, j: (0, gb0 + j)),
            pl.BlockSpec((1, COL_TILE), lambda i, j: (0, ga0 + j)),
            pl.BlockSpec((1, COL_TILE), lambda i, j: (0, gb0 + j)),
            resident, resident, resident,
        ],
        out_specs=act,
        out_shape=jax.ShapeDtypeStruct((s, D_MODEL), F32),
        compiler_params=_params(("arbitrary", "arbitrary")),
        name="merge",
    )(h, n, a, b, w_in, w_in, b_in, b_in, w_a, w_b, w_o)


def kernel(x, ffn1_norm, ffn1_w_in, ffn1_w_out, mix_norm, w_in, b_in, sgu_ln_g, sgu_ln_b, sgu_w_s, sgu_b_s, pool_w, pool_scale, w_branch_a, w_branch_b, w_out, ffn2_norm, ffn2_w_in, ffn2_w_out, final_norm):
    bsz, seq, d = x.shape
    assert d == D_MODEL and seq % ROW_TILE == 0 and (bsz * seq) % FFN_ROW_TILE == 0
    row2 = lambda v: v.reshape(1, -1)
    bf = lambda w: w.astype(BF16)

    h = x.reshape(bsz * seq, d)
    later_weights = (ffn2_w_in, ffn2_w_out, w_in, w_branch_a, w_branch_b, w_out)
    head, w1g_bf, w1u_bf, w1o_bf = _ffn_head(h, row2(ffn1_norm), ffn1_w_in, ffn1_w_out, row2(final_norm))
    h, ffn2_w_in_bf, ffn2_w_out_bf, w_in_bf, w_a_bf, w_b_bf, w_o_bf = _ffn(
        h, row2(ffn1_norm), w1g_bf, w1u_bf, 0, w1o_bf, row2(final_norm),
        final_norm=False, cast_on_side=later_weights, skip_first_tile=True)
    h = lax.dynamic_update_slice(h, head, (0, 0))
    b_in2 = row2(b_in)
    n, a, b = _mix(h, row2(mix_norm), w_in_bf, b_in2, row2(sgu_ln_g), row2(sgu_ln_b), sgu_w_s,
                   sgu_b_s[:, :, None], bf(pool_w), row2(pool_scale), seq=seq)
    h = _merge(h, n, a, b, w_in_bf, b_in2, w_a_bf, w_b_bf, w_o_bf)
    h, = _ffn(h, row2(ffn2_norm), ffn2_w_in_bf, ffn2_w_in_bf, D_FF // COL_TILE, ffn2_w_out_bf,
              row2(final_norm), final_norm=True)
    return h.reshape(bsz, seq, d).astype(x.dtype)
```

```python
import functools

import jax
import jax.numpy as jnp
from jax import lax
from jax.experimental import pallas as pl
from jax.experimental.pallas import tpu as pltpu

D_MODEL = 2048
D_FF = 5632
CHUNK = 64
SGU_BLOCK = 128
SGU_HEADS = 8
SGU_HEAD_DIM = D_MODEL // SGU_HEADS
POOL_WINDOWS = (2, 4, 8, 16)
POOL_GROUP_DIM = D_MODEL // len(POOL_WINDOWS)
POOL_HALO = 16
RMS_EPS = 1e-6
LN_EPS = 1e-5

ROW_TILE = 512
FFN_ROW_TILE = 1024
COL_TILE = 512
HEAD_COL_TILE = 256
V7X_VMEM_LIMIT_BYTES = 62 * 1024 * 1024

F32 = jnp.float32
BF16 = jnp.bfloat16
LANES = 128
BF16_SUBLANES = 16


def _side_cast_specs(shape, first_tile, n_tiles, n_steps):
    rows, cols = shape
    budget = (n_tiles - first_tile) * n_steps
    block = min(r for r in range(BF16_SUBLANES, rows + 1, BF16_SUBLANES)
                if rows % r == 0 and rows // r <= budget)
    index = lambda i, j: (jnp.clip((i - first_tile) * n_steps + j, 0, rows // block - 1), 0)
    return pl.BlockSpec((block, cols), index)


def _rms_bf16(x, g):
    return (x * lax.rsqrt(jnp.mean(x * x, axis=-1, keepdims=True) + RMS_EPS) * g).astype(BF16)


def _gelu(x):
    return 0.5 * x * (1.0 + lax.erf(x * (2.0 ** -0.5)))


def _dot(a, b):
    return jnp.dot(a, b, preferred_element_type=F32)


def _params(semantics):
    return pltpu.CompilerParams(dimension_semantics=semantics,
                                vmem_limit_bytes=V7X_VMEM_LIMIT_BYTES)


def _ffn_kernel(*refs, final_norm, n_side, weights_f32, skip_first_tile):
    n_emit = 3 if weights_f32 else 0
    x_ref, g_ref, wg_ref, wu_ref, wo_ref, fin_ref = refs[:6]
    side_in = refs[6:6 + n_side]
    o_ref = refs[6 + n_side]
    emit_out = refs[7 + n_side:7 + n_side + n_emit]
    side_out = refs[7 + n_side + n_emit:7 + 2 * n_side + n_emit]
    xn_ref = refs[-1]
    i, j = pl.program_id(0), pl.program_id(1)

    for src, dst in zip(side_in, side_out):
        dst[...] = src[...].astype(BF16)

    def chunk(xn):
        wg, wu, wo = wg_ref[...], wu_ref[...], wo_ref[...]
        if weights_f32:
            wg, wu, wo = wg.astype(BF16), wu.astype(BF16), wo.astype(BF16)
            for dst, w in zip(emit_out, (wg, wu, wo)):
                dst[...] = w
        gate = _dot(xn, wg)
        up = _dot(xn, wu)
        act = (0.5 * gate * jax.nn.sigmoid(gate) * up).astype(BF16)
        return _dot(act, wo)

    active = (i > 0) if skip_first_tile else True

    @pl.when((j == 0) & active)
    def _():
        x = x_ref[...]
        xn = _rms_bf16(x, g_ref[...])
        xn_ref[...] = xn
        o_ref[...] = x + chunk(xn)

    last = pl.num_programs(1) - 1
    middle = (j > 0) & ((j < last) if final_norm else True)

    @pl.when(middle & active)
    def _():
        o_ref[...] += chunk(xn_ref[...])

    if skip_first_tile:
        @pl.when((i == 0) & (j == 0))
        def _():
            o_ref[...] = jnp.zeros_like(o_ref)

    if final_norm:
        @pl.when(j == last)
        def _():
            half = FFN_ROW_TILE // 2
            for rs in (slice(0, half), slice(half, FFN_ROW_TILE)):
                h = o_ref[rs, :] + chunk(xn_ref[rs, :])
                o_ref[rs, :] = h * lax.rsqrt(jnp.mean(h * h, axis=-1, keepdims=True) + RMS_EPS) * fin_ref[...]


def _ffn(x, norm_g, w_gate, w_up, up_block, w_out, fin_g, *, final_norm, cast_on_side=(),
         skip_first_tile=False):
    s = x.shape[0]
    n_ff = D_FF // COL_TILE
    n_tiles = s // FFN_ROW_TILE
    row = lambda i, j: (i, 0)
    const = lambda i, j: (0, 0)
    first = int(skip_first_tile)
    step = (lambda i, j: jnp.where(i == 0, 0, j)) if skip_first_tile else (lambda i, j: j)
    side_specs = [_side_cast_specs(w.shape, first, n_tiles, n_ff) for w in cast_on_side]
    return pl.pallas_call(
        functools.partial(_ffn_kernel, final_norm=final_norm, n_side=len(cast_on_side),
                          weights_f32=False, skip_first_tile=skip_first_tile),
        grid=(n_tiles, n_ff),
        in_specs=[
            pl.BlockSpec((FFN_ROW_TILE, D_MODEL), lambda i, j: (jnp.maximum(i, first), 0)),
            pl.BlockSpec((1, D_MODEL), const),
            pl.BlockSpec((D_MODEL, COL_TILE), lambda i, j: (0, step(i, j))),
            pl.BlockSpec((D_MODEL, COL_TILE), lambda i, j: (0, step(i, j) + up_block)),
            pl.BlockSpec((COL_TILE, D_MODEL), lambda i, j: (step(i, j), 0)),
            pl.BlockSpec((1, D_MODEL), const),
        ] + side_specs,
        out_specs=[pl.BlockSpec((FFN_ROW_TILE, D_MODEL), row)] + side_specs,
        out_shape=[jax.ShapeDtypeStruct((s, D_MODEL), F32)]
                  + [jax.ShapeDtypeStruct(w.shape, BF16) for w in cast_on_side],
        scratch_shapes=[pltpu.VMEM((FFN_ROW_TILE, D_MODEL), BF16)],
        compiler_params=_params(("arbitrary", "arbitrary")),
        name="ffn_final" if final_norm else "ffn",
    )(x, norm_g, w_gate, w_up, w_out, fin_g, *cast_on_side)


def _ffn_head(x, norm_g, w_in, w_out, fin_g):
    n_ff = D_FF // HEAD_COL_TILE
    const = lambda i, j: (0, 0)
    gate = pl.BlockSpec((D_MODEL, HEAD_COL_TILE), lambda i, j: (0, j))
    down = pl.BlockSpec((HEAD_COL_TILE, D_MODEL), lambda i, j: (j, 0))
    tile = pl.BlockSpec((FFN_ROW_TILE, D_MODEL), const)
    return pl.pallas_call(
        functools.partial(_ffn_kernel, final_norm=False, n_side=0, weights_f32=True,
                          skip_first_tile=False),
        grid=(1, n_ff),
        in_specs=[
            pl.BlockSpec((FFN_ROW_TILE, D_MODEL), const, pipeline_mode=pl.Buffered(1)),
            pl.BlockSpec((1, D_MODEL), const),
            gate,
            pl.BlockSpec((D_MODEL, HEAD_COL_TILE), lambda i, j: (0, j + n_ff)),
            down,
            pl.BlockSpec((1, D_MODEL), const),
        ],
        out_specs=[tile, gate, gate, down],
        out_shape=[jax.ShapeDtypeStruct((FFN_ROW_TILE, D_MODEL), F32),
                   jax.ShapeDtypeStruct((D_MODEL, D_FF), BF16),
                   jax.ShapeDtypeStruct((D_MODEL, D_FF), BF16),
                   jax.ShapeDtypeStruct((D_FF, D_MODEL), BF16)],
        scratch_shapes=[pltpu.VMEM((FFN_ROW_TILE, D_MODEL), BF16)],
        compiler_params=_params(("arbitrary", "arbitrary")),
        name="ffn_head",
    )(x, norm_g, w_in, w_in, w_out, fin_g)


def _mix_kernel(h_ref, g_ref, w_ref, b_ref, lng_ref, lnb_ref, ws_ref, bs_ref, pw_ref, ps_ref,
                n_ref, a_ref, p_ref, v_ref, z_ref, *, seq):
    i, j = pl.program_id(0), pl.program_id(1)

    @pl.when(j == 0)
    def _():
        n = _rms_bf16(h_ref[...], g_ref[...])
        n_ref[...] = n
        v_ref[...] = _gelu(_dot(n, w_ref[...]) + b_ref[...])

    @pl.when(j == 1)
    def _():
        n = n_ref[...]
        v = v_ref[...]
        vc = v - jnp.mean(v, axis=-1, keepdims=True)
        rstd = lax.rsqrt(jnp.mean(jnp.square(vc), axis=-1, keepdims=True) + LN_EPS)
        vln = (vc * rstd * lng_ref[...] + lnb_ref[...]).astype(BF16)
        pi = lax.broadcasted_iota(jnp.int32, (SGU_BLOCK, SGU_BLOCK), 0) // CHUNK
        pj = lax.broadcasted_iota(jnp.int32, (SGU_BLOCK, SGU_BLOCK), 1) // CHUNK
        mask = (pj <= pi).astype(F32)
        for head in range(SGU_HEADS):
            cs = slice(head * SGU_HEAD_DIM, (head + 1) * SGU_HEAD_DIM)
            u = _gelu(_dot(n, w_ref[:, cs]) + b_ref[:, cs])
            w = (ws_ref[head] * mask).astype(BF16)
            bias = bs_ref[head]
            for blk in range(ROW_TILE // SGU_BLOCK):
                rs = slice(blk * SGU_BLOCK, (blk + 1) * SGU_BLOCK)
                sp = _dot(w, vln[rs, cs]) + bias
                a_ref[rs, cs] = (u[rs] * sp).astype(BF16)

    t0 = (i * ROW_TILE) % seq

    @pl.when((j == 2) & (t0 == 0))
    def _():
        z_ref[0:POOL_HALO, :] = jnp.zeros((POOL_HALO, D_MODEL), F32)

    @pl.when(j == 2)
    def _():
        t = t0 + lax.broadcasted_iota(jnp.int32, (ROW_TILE, 1), 0)
        z_ref[POOL_HALO:, :] = _dot(n_ref[...], w_ref[...]) + b_ref[...]
        for k, win in reversed(list(enumerate(POOL_WINDOWS))):
            cs = slice(k * POOL_GROUP_DIM, (k + 1) * POOL_GROUP_DIM)
            ext = z_ref[:, cs]
            acc = ext
            d = 1
            while d < win:
                acc = acc + pltpu.roll(acc, d, axis=0)
                d *= 2
            inv_cnt = 1.0 / jnp.minimum(t + 1, win).astype(F32)
            pooled = (acc[POOL_HALO:] * inv_cnt - ext[POOL_HALO:]).astype(BF16)
            mixed = _dot(pooled, pw_ref[k])
            p_ref[:, cs] = (mixed * ps_ref[:, cs]).astype(BF16)
        z_ref[0:POOL_HALO, :] = z_ref[ROW_TILE:ROW_TILE + POOL_HALO, :]


def _mix(h, norm_g, w_in, b_in, ln_g, ln_b, w_s, b_s, pool_w, pool_scale, *, seq):
    s = h.shape[0]
    row = lambda i, j: (i, 0)
    const = lambda i, j: (0, 0)
    const3 = lambda i, j: (0, 0, 0)
    wcol = lambda i, j: (0, jnp.where(j < 2, 1 - j, j))
    act = pl.BlockSpec((ROW_TILE, D_MODEL), row)
    return pl.pallas_call(
        functools.partial(_mix_kernel, seq=seq),
        grid=(s // ROW_TILE, 3),
        in_specs=[
            act,
            pl.BlockSpec((1, D_MODEL), const),
            pl.BlockSpec((D_MODEL, D_MODEL), wcol),
            pl.BlockSpec((1, D_MODEL), wcol),
            pl.BlockSpec((1, D_MODEL), const),
            pl.BlockSpec((1, D_MODEL), const),
            pl.BlockSpec((SGU_HEADS, SGU_BLOCK, SGU_BLOCK), const3),
            pl.BlockSpec((SGU_HEADS, SGU_BLOCK, 1), const3),
            pl.BlockSpec((len(POOL_WINDOWS), POOL_GROUP_DIM, POOL_GROUP_DIM), const3),
            pl.BlockSpec((1, D_MODEL), const),
        ],
        out_specs=[act, act, act],
        out_shape=[jax.ShapeDtypeStruct((s, D_MODEL), BF16)] * 3,
        scratch_shapes=[pltpu.VMEM((ROW_TILE, D_MODEL), F32),
                        pltpu.VMEM((ROW_TILE + POOL_HALO, D_MODEL), F32)],
        compiler_params=_params(("arbitrary", "arbitrary")),
        name="mix",
    )(h, norm_g, w_in, b_in, ln_g, ln_b, w_s, b_s, pool_w, pool_scale)


def _merge_kernel(h_ref, n_ref, a_ref, b_ref, wga_ref, wgb_ref, bga_ref, bgb_ref,
                  wa_ref, wb_ref, wo_ref, o_ref):
    j = pl.program_id(1)
    col = pl.multiple_of(j * COL_TILE, COL_TILE)

    def chunk():
        n = n_ref[...]
        gate_a = jax.nn.sigmoid(_dot(n, wga_ref[...]) + bga_ref[...])
        gate_b = jax.nn.sigmoid(_dot(n, wgb_ref[...]) + bgb_ref[...])
        y_a = _dot(a_ref[...], wa_ref[:, pl.ds(col, COL_TILE)])
        y_b = _dot(b_ref[...], wb_ref[:, pl.ds(col, COL_TILE)])
        merged = (gate_a * y_a + gate_b * y_b).astype(BF16)
        return _dot(merged, wo_ref[pl.ds(col, COL_TILE), :])

    @pl.when(j == 0)
    def _():
        o_ref[...] = chunk()
        o_ref[:, pl.ds(col, COL_TILE)] += h_ref[...]

    @pl.when(j > 0)
    def _():
        o_ref[...] += chunk()
        o_ref[:, pl.ds(col, COL_TILE)] += h_ref[...]


def _merge(h, n, a, b, w_in, b_in, w_a, w_b, w_o):
    s = h.shape[0]
    n_c = D_MODEL // COL_TILE
    ga0 = 3 * D_MODEL // COL_TILE
    gb0 = 4 * D_MODEL // COL_TILE
    row = lambda i, j: (i, 0)
    act = pl.BlockSpec((ROW_TILE, D_MODEL), row)
    resident = pl.BlockSpec((D_MODEL, D_MODEL), lambda i, j: (0, 0), pipeline_mode=pl.Buffered(1))
    return pl.pallas_call(
        _merge_kernel,
        grid=(s // ROW_TILE, n_c),
        in_specs=[
            pl.BlockSpec((ROW_TILE, COL_TILE), lambda i, j: (i, j)),
            act, act, act,
            pl.BlockSpec((D_MODEL, COL_TILE), lambda i, j: (0, ga0 + j)),
            pl.BlockSpec((D_MODEL, COL_TILE), lambda i, j: (0, gb0 + j)),
            pl.BlockSpec((1, COL_TILE), lambda i, j: (0, ga0 + j)),
            pl.BlockSpec((1, COL_TILE), lambda i, j: (0, gb0 + j)),
            resident, resident, resident,
        ],
        out_specs=act,
        out_shape=jax.ShapeDtypeStruct((s, D_MODEL), F32),
        compiler_params=_params(("arbitrary", "arbitrary")),
        name="merge",
    )(h, n, a, b, w_in, w_in, b_in, b_in, w_a, w_b, w_o)


def kernel(x, ffn1_norm, ffn1_w_in, ffn1_w_out, mix_norm, w_in, b_in, sgu_ln_g, sgu_ln_b, sgu_w_s, sgu_b_s, pool_w, pool_scale, w_branch_a, w_branch_b, w_out, ffn2_norm, ffn2_w_in, ffn2_w_out, final_norm):
    bsz, seq, d = x.shape
    assert d == D_MODEL and seq % ROW_TILE == 0 and (bsz * seq) % FFN_ROW_TILE == 0
    row2 = lambda v: v.reshape(1, -1)
    bf = lambda w: w.astype(BF16)

    h = x.reshape(bsz * seq, d)
    later_weights = (ffn2_w_in, ffn2_w_out, w_in, w_branch_a, w_branch_b, w_out)
    head, w1g_bf, w1u_bf, w1o_bf = _ffn_head(h, row2(ffn1_norm), ffn1_w_in, ffn1_w_out, row2(final_norm))
    h, ffn2_w_in_bf, ffn2_w_out_bf, w_in_bf, w_a_bf, w_b_bf, w_o_bf = _ffn(
        h, row2(ffn1_norm), w1g_bf, w1u_bf, 0, w1o_bf, row2(final_norm),
        final_norm=False, cast_on_side=later_weights, skip_first_tile=True)
    h = lax.dynamic_update_slice(h, head, (0, 0))
    b_in2 = row2(b_in)
    n, a, b = _mix(h, row2(mix_norm), w_in_bf, b_in2, row2(sgu_ln_g), row2(sgu_ln_b), sgu_w_s,
                   sgu_b_s[:, :, None], bf(pool_w), row2(pool_scale), seq=seq)
    h = _merge(h, n, a, b, w_in_bf, b_in2, w_a_bf, w_b_bf, w_o_bf)
    h, = _ffn(h, row2(ffn2_norm), ffn2_w_in_bf, ffn2_w_in_bf, D_FF // COL_TILE, ffn2_w_out_bf,
              row2(final_norm), final_norm=True)
    return h.reshape(bsz, seq, d).astype(x.dtype)
```
